```python
import math
import jax, jax.numpy as jnp
from jax import lax
import numpy as np

D_MODEL = 1024
BATCH = 2
SEQ = 8192
DEPTH = 4

D_RNN = 1536
LRU_HEADS = 16
LRU_HEAD_DIM = D_RNN // LRU_HEADS
CONV_WIDTH = 4
LRU_C = 8.0
D_S5 = D_MODEL
S5_GROUP = 16
S5_GROUPS = D_S5 // S5_GROUP
S5_STATE = 64
S5_CHUNK = 128
SPLITS = (D_RNN, 2 * D_RNN, 2 * D_RNN + D_S5, 2 * D_RNN + D_S5 + D_MODEL)
D_IN = 2 * D_RNN + D_S5 + 2 * D_MODEL
N_EXPERTS = 64
TOP_K = 8
N_EXPERT_GROUPS = 8
TOPK_GROUPS = 4
D_EXPERT = 256
ROUTED_SCALE = 2.5
EXPERT_CHUNK = 8
D_PLE = 256
LN_EPS = 1e-5
ALPHA = (2 * DEPTH) ** 0.25
BETA = (8 * DEPTH) ** -0.25

kernel_name = 'hybrid_rglru_s5_moe_deepnorm'


def layer_norm(x, g, b):
    xf = x.astype(jnp.float32)
    mu = xf.mean(-1, keepdims=True)
    var = jnp.square(xf - mu).mean(-1, keepdims=True)
    y = (xf - mu) * lax.rsqrt(var + LN_EPS)
    return (y * g.astype(jnp.float32) + b.astype(jnp.float32)).astype(x.dtype)


def causal_depthwise_conv(x, w, b):
    c = x.shape[-1]
    y = lax.conv_general_dilated(x, w[:, None, :], window_strides=(1,),
                                 padding=[(CONV_WIDTH - 1, 0)],
                                 dimension_numbers=('NWC', 'WIO', 'NWC'),
                                 feature_group_count=c)
    return y + b


def _linear_op(e1, e2):
    a1, b1 = e1
    a2, b2 = e2
    return a2 * a1, a2 * b1 + b2


def rg_lru(x, wa, ba, wx, bx, lam):
    bsz, seq, _ = x.shape
    f32 = jnp.float32
    xf = x.astype(f32)
    xh = xf.reshape(bsz, seq, LRU_HEADS, LRU_HEAD_DIM)
    r = jax.nn.sigmoid(jnp.einsum('blhi,hij->blhj', xh, wa.astype(f32)) + ba.astype(f32)).reshape(bsz, seq, D_RNN)
    i = jax.nn.sigmoid(jnp.einsum('blhi,hij->blhj', xh, wx.astype(f32)) + bx.astype(f32)).reshape(bsz, seq, D_RNN)
    log_a = -LRU_C * r * jax.nn.softplus(-lam.astype(f32))
    a = jnp.exp(log_a)
    mult = jnp.sqrt(-jnp.expm1(2.0 * log_a))
    mult = jnp.where(jnp.arange(seq)[None, :, None] == 0, 1.0, mult)
    b = mult * (i * xf)
    _, h = lax.associative_scan(_linear_op, (a, b), axis=1)
    return h.astype(x.dtype)


def _complex_op(e1, e2):
    a1r, a1i, b1r, b1i = e1
    a2r, a2i, b2r, b2i = e2
    ar = a2r * a1r - a2i * a1i
    ai = a2r * a1i + a2i * a1r
    br = a2r * b1r - a2i * b1i + b2r
    bi = a2r * b1i + a2i * b1r + b2i
    return ar, ai, br, bi


def s5_ssm(u, a_re, a_im, b_re, b_im, c_re, c_im, d, log_dt):
    bsz, seq, _ = u.shape
    f32 = jnp.float32
    uf = u.astype(f32)
    ar, ai = a_re.astype(f32), a_im.astype(f32)
    dt = jnp.exp(log_dt.astype(f32))[:, None]
    mag, ph = jnp.exp(dt * ar), dt * ai
    abar_re, abar_im = mag * jnp.cos(ph), mag * jnp.sin(ph)
    den = ar * ar + ai * ai
    z_re = ((abar_re - 1.0) * ar + abar_im * ai) / den
    z_im = (abar_im * ar - (abar_re - 1.0) * ai) / den
    br, bi = b_re.astype(f32), b_im.astype(f32)
    bb_re = z_re[..., None] * br - z_im[..., None] * bi
    bb_im = z_re[..., None] * bi + z_im[..., None] * br
    cr, ci = c_re.astype(f32), c_im.astype(f32)
    steps = jnp.arange(1, S5_CHUNK + 1, dtype=f32)[:, None, None]
    pmag, pph = jnp.exp(steps * dt * ar), steps * dt * ai
    pw_re, pw_im = pmag * jnp.cos(pph), pmag * jnp.sin(pph)
    blk = (bsz, S5_CHUNK, S5_GROUPS, S5_STATE)
    a_blk_re = jnp.broadcast_to(abar_re, blk)
    a_blk_im = jnp.broadcast_to(abar_im, blk)
    n_chunks = seq // S5_CHUNK
    u_chunks = uf.reshape(bsz, n_chunks, S5_CHUNK, S5_GROUPS, S5_GROUP).transpose(1, 0, 2, 3, 4)

    def chunk_step(carry, uc):
        h_re, h_im = carry
        bu_re = jnp.einsum('btgp,gnp->btgn', uc, bb_re)
        bu_im = jnp.einsum('btgp,gnp->btgn', uc, bb_im)
        _, _, hl_re, hl_im = lax.associative_scan(_complex_op, (a_blk_re, a_blk_im, bu_re, bu_im), axis=1)
        hr = hl_re + pw_re * h_re[:, None] - pw_im * h_im[:, None]
        hi = hl_im + pw_re * h_im[:, None] + pw_im * h_re[:, None]
        y = jnp.einsum('gpn,btgn->btgp', cr, hr) - jnp.einsum('gpn,btgn->btgp', ci, hi)
        return (hr[:, -1], hi[:, -1]), y

    init = (jnp.zeros((bsz, S5_GROUPS, S5_STATE), f32), jnp.zeros((bsz, S5_GROUPS, S5_STATE), f32))
    _, ys = lax.scan(chunk_step, init, u_chunks)
    y = ys.transpose(1, 0, 2, 3, 4).reshape(bsz, seq, D_S5)
    y = y + d.astype(f32) * uf
    return y.astype(u.dtype)


def token_mixer(h, w_in, conv_w, conv_b, wa, ba, wx, bx, lam, w_lru_out,
                a_re, a_im, b_re, b_im, c_re, c_im, d, log_dt, w_glu, b_glu, w_out):
    z = jnp.einsum('bld,de->ble', h, w_in)
    x_lru, g_lru, u_s5, gate_a, gate_b = jnp.split(z, SPLITS, axis=-1)
    y_a = jax.nn.gelu(g_lru) * rg_lru(causal_depthwise_conv(x_lru, conv_w, conv_b), wa, ba, wx, bx, lam)
    y_a = jnp.einsum('blr,rd->bld', y_a, w_lru_out)
    y_s = jax.nn.gelu(s5_ssm(u_s5, a_re, a_im, b_re, b_im, c_re, c_im, d, log_dt))
    glu = jnp.einsum('bls,se->ble', y_s, w_glu) + b_glu
    lin, gl = jnp.split(glu, 2, axis=-1)
    y_b = lin * jax.nn.sigmoid(gl)
    merged = jax.nn.sigmoid(gate_a) * y_a + jax.nn.sigmoid(gate_b) * y_b
    return jnp.einsum('bld,de->ble', merged, w_out)


def routed_moe(x, router_w, router_bias, w_gate, w_up, w_down, ws_gate, ws_up, ws_down):
    bsz, seq, d = x.shape
    xt = x.reshape(bsz * seq, d)
    f32 = jnp.float32
    scores = jax.nn.sigmoid(jnp.einsum('td,de->te', xt.astype(f32), router_w.astype(f32)))
    biased = scores + router_bias.astype(f32)
    per_group = N_EXPERTS // N_EXPERT_GROUPS
    grouped = biased.reshape(-1, N_EXPERT_GROUPS, per_group)
    group_score = lax.top_k(grouped, 2)[0].sum(-1)
    _, group_idx = lax.top_k(group_score, TOPK_GROUPS)
    group_mask = jax.nn.one_hot(group_idx, N_EXPERT_GROUPS, dtype=f32).sum(1)
    expert_mask = jnp.repeat(group_mask, per_group, axis=1)
    masked = jnp.where(expert_mask > 0, biased, -jnp.inf)
    _, expert_idx = lax.top_k(masked, TOP_K)
    gate = jnp.take_along_axis(scores, expert_idx, axis=-1)
    gate = ROUTED_SCALE * gate / gate.sum(-1, keepdims=True)
    combine = jnp.einsum('tk,tke->te', gate, jax.nn.one_hot(expert_idx, N_EXPERTS, dtype=f32)).astype(x.dtype)
    y = jnp.einsum('tf,fd->td', jax.nn.silu(xt @ ws_gate) * (xt @ ws_up), ws_down)
    for c in range(N_EXPERTS // EXPERT_CHUNK):
        sl = slice(c * EXPERT_CHUNK, (c + 1) * EXPERT_CHUNK)
        g = jnp.einsum('td,edf->tef', xt, w_gate[sl])
        u = jnp.einsum('td,edf->tef', xt, w_up[sl])
        hdn = jax.nn.silu(g) * u * combine[:, sl, None]
        y = y + jnp.einsum('tef,efd->td', hdn, w_down[sl])
    return y.reshape(bsz, seq, d)


def setup_inputs(seed: int = 0) -> dict:
    key = jax.random.key(seed)
    ks = iter(jax.random.split(key, 48))
    f32 = jnp.float32

    def nrm(shape, scale):
        return jax.random.normal(next(ks), shape, f32) * scale

    L = DEPTH
    x = nrm((BATCH, SEQ, D_MODEL), 1.0)
    p = nrm((DEPTH, BATCH, SEQ, D_PLE), 1.0)
    w_in = nrm((L, D_MODEL, D_IN), D_MODEL ** -0.5)
    conv_w = nrm((L, CONV_WIDTH, D_RNN), CONV_WIDTH ** -0.5)
    conv_b = nrm((L, D_RNN), 0.02)
    lru_wa = nrm((L, LRU_HEADS, LRU_HEAD_DIM, LRU_HEAD_DIM), LRU_HEAD_DIM ** -0.5)
    lru_ba = nrm((L, LRU_HEADS, LRU_HEAD_DIM), 0.02)
    lru_wx = nrm((L, LRU_HEADS, LRU_HEAD_DIM, LRU_HEAD_DIM), LRU_HEAD_DIM ** -0.5)
    lru_bx = nrm((L, LRU_HEADS, LRU_HEAD_DIM), 0.02)
    a_c = jax.random.uniform(next(ks), (L, D_RNN), f32, 0.9, 0.999)
    a0 = a_c ** (1.0 / LRU_C)
    lru_lambda = jnp.log(a0) - jnp.log1p(-a0)
    w_lru_out = nrm((L, D_RNN, D_MODEL), D_RNN ** -0.5)
    s5_a_re = -0.5 * (1.0 + nrm((L, S5_GROUPS, S5_STATE), 0.02))
    s5_a_im = math.pi * jnp.arange(S5_STATE, dtype=f32) + nrm((L, S5_GROUPS, S5_STATE), 0.01)
    s5_b_re = nrm((L, S5_GROUPS, S5_STATE, S5_GROUP), (2 * S5_GROUP) ** -0.5)
    s5_b_im = nrm((L, S5_GROUPS, S5_STATE, S5_GROUP), (2 * S5_GROUP) ** -0.5)
    s5_c_re = nrm((L, S5_GROUPS, S5_GROUP, S5_STATE), S5_STATE ** -0.5)
    s5_c_im = nrm((L, S5_GROUPS, S5_GROUP, S5_STATE), S5_STATE ** -0.5)
    s5_d = nrm((L, D_S5), 0.5)
    s5_log_dt = jax.random.uniform(next(ks), (L, S5_GROUPS), f32, math.log(0.001), math.log(0.1))
    w_glu = nrm((L, D_S5, 2 * D_MODEL), D_S5 ** -0.5)
    b_glu = nrm((L, 2 * D_MODEL), 0.02)
    w_out = nrm((L, D_MODEL, D_MODEL), BETA * D_MODEL ** -0.5)
    ln1_g = 1.0 + nrm((L, D_MODEL), 0.02)
    ln1_b = nrm((L, D_MODEL), 0.02)
    router_w = nrm((L, D_MODEL, N_EXPERTS), D_MODEL ** -0.5)
    router_bias = nrm((L, N_EXPERTS), 0.01)
    moe_w_gate = nrm((L, N_EXPERTS, D_MODEL, D_EXPERT), D_MODEL ** -0.5)
    moe_w_up = nrm((L, N_EXPERTS, D_MODEL, D_EXPERT), D_MODEL ** -0.5)
    moe_w_down = nrm((L, N_EXPERTS, D_EXPERT, D_MODEL), BETA * D_EXPERT ** -0.5)
    shared_w_gate = nrm((L, D_MODEL, D_EXPERT), D_MODEL ** -0.5)
    shared_w_up = nrm((L, D_MODEL, D_EXPERT), D_MODEL ** -0.5)
    shared_w_down = nrm((L, D_EXPERT, D_MODEL), BETA * D_EXPERT ** -0.5)
    ple_w = nrm((L, D_PLE, D_MODEL), BETA * D_PLE ** -0.5)
    ple_gate_w = nrm((L, D_MODEL, D_MODEL), D_MODEL ** -0.5)
    ple_gate_b = nrm((L, D_MODEL), 0.02)
    ln2_g = 1.0 + nrm((L, D_MODEL), 0.02)
    ln2_b = nrm((L, D_MODEL), 0.02)
    return {'x': x, 'p': p, 'w_in': w_in, 'conv_w': conv_w, 'conv_b': conv_b,
            'lru_wa': lru_wa, 'lru_ba': lru_ba, 'lru_wx': lru_wx, 'lru_bx': lru_bx,
            'lru_lambda': lru_lambda, 'w_lru_out': w_lru_out,
            's5_a_re': s5_a_re, 's5_a_im': s5_a_im, 's5_b_re': s5_b_re, 's5_b_im': s5_b_im,
            's5_c_re': s5_c_re, 's5_c_im': s5_c_im, 's5_d': s5_d, 's5_log_dt': s5_log_dt,
            'w_glu': w_glu, 'b_glu': b_glu, 'w_out': w_out, 'ln1_g': ln1_g, 'ln1_b': ln1_b,
            'router_w': router_w, 'router_bias': router_bias,
            'moe_w_gate': moe_w_gate, 'moe_w_up': moe_w_up, 'moe_w_down': moe_w_down,
            'shared_w_gate': shared_w_gate, 'shared_w_up': shared_w_up, 'shared_w_down': shared_w_down,
            'ple_w': ple_w, 'ple_gate_w': ple_gate_w, 'ple_gate_b': ple_gate_b,
            'ln2_g': ln2_g, 'ln2_b': ln2_b}


def reference(x, p, w_in, conv_w, conv_b, lru_wa, lru_ba, lru_wx, lru_bx, lru_lambda, w_lru_out,
              s5_a_re, s5_a_im, s5_b_re, s5_b_im, s5_c_re, s5_c_im, s5_d, s5_log_dt,
              w_glu, b_glu, w_out, ln1_g, ln1_b, router_w, router_bias,
              moe_w_gate, moe_w_up, moe_w_down, shared_w_gate, shared_w_up, shared_w_down,
              ple_w, ple_gate_w, ple_gate_b, ln2_g, ln2_b):
    for i in range(DEPTH):
        mix = token_mixer(x, w_in[i], conv_w[i], conv_b[i], lru_wa[i], lru_ba[i], lru_wx[i], lru_bx[i],
                          lru_lambda[i], w_lru_out[i], s5_a_re[i], s5_a_im[i], s5_b_re[i], s5_b_im[i],
                          s5_c_re[i], s5_c_im[i], s5_d[i], s5_log_dt[i], w_glu[i], b_glu[i], w_out[i])
        x = layer_norm(ALPHA * x + mix, ln1_g[i], ln1_b[i])
        ffn = routed_moe(x, router_w[i], router_bias[i], moe_w_gate[i], moe_w_up[i], moe_w_down[i],
                         shared_w_gate[i], shared_w_up[i], shared_w_down[i])
        ple = jax.nn.sigmoid(jnp.einsum('bld,de->ble', x, ple_gate_w[i]) + ple_gate_b[i]) * \
            jnp.einsum('blq,qd->bld', p[i], ple_w[i])
        x = layer_norm(ALPHA * x + ffn + ple, ln2_g[i], ln2_b[i])
    return x
```

```python
import functools
import math

import jax
import jax.numpy as jnp
from jax import lax
from jax.experimental import pallas as pl
from jax.experimental.pallas import tpu as pltpu

F32 = jnp.float32
BF16 = jnp.bfloat16

LRU_C = 8.0
TOP_K = 8
N_EXPERT_GROUPS = 8
TOPK_GROUPS = 4
ROUTED_SCALE = 2.5
LN_EPS = 1e-5

LANES = 128
SUBLANES = 8
VMEM_LIMIT_BYTES = 56 * 1024 * 1024

S5_SUB = 16
S5_LANE_GROUPS = 8


def _params(*sem):
    return pltpu.CompilerParams(dimension_semantics=sem, vmem_limit_bytes=VMEM_LIMIT_BYTES)


def _const_spec(shape):
    nd = len(shape)
    return pl.BlockSpec(shape, lambda *_: (0,) * nd, pipeline_mode=pl.Buffered(1))


def _gelu(x):
    return 0.5 * x * (1.0 + jnp.tanh(math.sqrt(2.0 / math.pi) * (x + 0.044715 * (x * x * x))))


def _sigmoid(x):
    return 1.0 / (1.0 + jnp.exp(-x))


def _layer_norm(v, g, b):
    mu = jnp.mean(v, axis=-1, keepdims=True)
    c = v - mu
    var = jnp.mean(c * c, axis=-1, keepdims=True)
    return c * lax.rsqrt(var + LN_EPS) * g + b


def _inproj_kernel(x_ref, w_ref, *o_refs, bounds):
    xb = x_ref[...].astype(BF16)
    for o_ref, (lo, hi) in zip(o_refs, bounds):
        z = jnp.dot(xb, w_ref[:, lo:hi], preferred_element_type=F32).astype(o_ref.dtype)
        if len(o_ref.shape) == 3:
            for c in range(o_ref.shape[0]):
                o_ref[c] = z[:, c * LANES:(c + 1) * LANES]
        else:
            o_ref[...] = z


def _inproj(x, w_bf16, widths, tiled, tm):
    t, d = x.shape
    bounds, lo = [], 0
    for w in widths:
        bounds.append((lo, lo + w))
        lo += w
    out_specs, out_shape = [], []
    for w, tl in zip(widths, tiled):
        if tl:
            out_specs.append(pl.BlockSpec((w // LANES, tm, LANES), lambda i: (0, i, 0)))
            out_shape.append(jax.ShapeDtypeStruct((w // LANES, t, LANES), BF16))
        else:
            out_specs.append(pl.BlockSpec((tm, w), lambda i: (i, 0)))
            out_shape.append(jax.ShapeDtypeStruct((t, w), BF16))
    return pl.pallas_call(
        functools.partial(_inproj_kernel, bounds=tuple(bounds)),
        grid=(t // tm,),
        in_specs=[pl.BlockSpec((tm, d), lambda i: (i, 0)), _const_spec(w_bf16.shape)],
        out_specs=out_specs,
        out_shape=out_shape,
        compiler_params=_params("parallel"),
        name="inproj",
    )(x, w_bf16)


def _scan_rows(a, b):
    n = a.shape[0]
    row = lax.broadcasted_iota(jnp.int32, a.shape, 0)
    d = 1
    while d < n:
        a_s = pltpu.roll(a, d, 0)
        b_s = pltpu.roll(b, d, 0)
        m = row >= d
        b = jnp.where(m, a * b_s + b, b)
        a = jnp.where(m, a * a_s, a)
        d *= 2
    return a, b


def _lru_kernel(x_ref, g_ref, ga_ref, cw_ref, cb_ref, wa_ref, ba_ref, wx_ref, bx_ref, lam_ref, wo_ref,
                o_ref, prev_ref, h_ref, *, n_blk, blk):
    i = pl.program_id(1)

    @pl.when(i == 0)
    def _():
        prev_ref[...] = jnp.zeros_like(prev_ref)
        h_ref[...] = jnp.zeros_like(h_ref)

    tm = x_ref.shape[0]
    x = x_ref[...].astype(F32)
    ext = jnp.concatenate([prev_ref[...], x], axis=0)
    prev_ref[...] = x[tm - SUBLANES:, :]
    kw = cw_ref.shape[0]
    xc = cb_ref[...] + cw_ref[kw - 1:kw, :] * x
    for j in range(1, kw):
        xc = xc + cw_ref[kw - 1 - j:kw - j, :] * pltpu.roll(ext, j, 0)[SUBLANES:, :]

    xcb = xc.astype(BF16)
    r_parts, i_parts = [], []
    for k in range(n_blk):
        xk = xcb[:, k * blk:(k + 1) * blk]
        r_parts.append(jnp.dot(xk, wa_ref[k], preferred_element_type=F32))
        i_parts.append(jnp.dot(xk, wx_ref[k], preferred_element_type=F32))
    r = _sigmoid(jnp.concatenate(r_parts, axis=1) + ba_ref[...])
    ig = _sigmoid(jnp.concatenate(i_parts, axis=1) + bx_ref[...])

    nl = -lam_ref[...]
    softplus = jnp.maximum(nl, 0.0) + jnp.log1p(jnp.exp(-jnp.abs(nl)))
    log_a = (-LRU_C) * r * softplus
    a = jnp.exp(log_a)
    mult = jnp.sqrt(1.0 - a * a)
    row = lax.broadcasted_iota(jnp.int32, a.shape, 0)
    mult = jnp.where(jnp.logical_and(row == 0, i == 0), 1.0, mult)
    b = mult * (ig * xc)

    a_cum, h_loc = _scan_rows(a, b)
    h = h_loc + a_cum * h_ref[0:1, :]
    h_ref[...] = jnp.broadcast_to(h[tm - 1:tm, :], h_ref.shape)

    y = (_gelu(g_ref[...].astype(F32)) * h).astype(BF16)
    ya = jnp.dot(y, wo_ref[...], preferred_element_type=F32)
    o_ref[...] = (_sigmoid(ga_ref[...].astype(F32)) * ya).astype(o_ref.dtype)


def _lru_branch(x_lru, g_lru, gate_a, conv_w, conv_b, wa_blk, ba, wx_blk, bx, lam, wo_bf16, bsz, tm):
    t, c = x_lru.shape
    d = gate_a.shape[1]
    seq = t // bsz
    nt = seq // tm
    n_blk, blk, _ = wa_blk.shape
    row = lambda b, i: (b * nt + i, 0)
    return pl.pallas_call(
        functools.partial(_lru_kernel, n_blk=n_blk, blk=blk),
        grid=(bsz, nt),
        in_specs=[pl.BlockSpec((tm, c), row), pl.BlockSpec((tm, c), row), pl.BlockSpec((tm, d), row),
                  _const_spec(conv_w.shape), _const_spec((1, c)),
                  _const_spec(wa_blk.shape), _const_spec((1, c)),
                  _const_spec(wx_blk.shape), _const_spec((1, c)),
                  _const_spec((1, c)), _const_spec(wo_bf16.shape)],
        out_specs=pl.BlockSpec((tm, d), row),
        out_shape=jax.ShapeDtypeStruct((t, d), BF16),
        scratch_shapes=[pltpu.VMEM((SUBLANES, c), F32), pltpu.VMEM((SUBLANES, c), F32)],
        compiler_params=_params("arbitrary", "arbitrary"),
        name="lru_branch",
    )(x_lru, g_lru, gate_a, conv_w, conv_b.reshape(1, c), wa_blk, ba.reshape(1, c),
      wx_blk, bx.reshape(1, c), lam.reshape(1, c), wo_bf16)


def _block_diag_heads(w, heads_per_blk):
    h, dh, _ = w.shape
    nb = h // heads_per_blk
    eye = jnp.eye(heads_per_blk, dtype=w.dtype)
    wb = w.reshape(nb, heads_per_blk, dh, dh)
    out = jnp.einsum("nhij,hk->nhikj", wb, eye)
    return out.reshape(nb, heads_per_blk * dh, heads_per_blk * dh).astype(BF16)


def _s5_lag_kernel(cr_ref, ci_ref, br_ref, bi_ref, pr_ref, pi_ref, k_ref):
    cr, ci = cr_ref[0], ci_ref[0]
    br, bi = br_ref[0], bi_ref[0]
    for j in range(k_ref.shape[1]):
        pr, pim = pr_ref[0, j:j + 1, :], pi_ref[0, j:j + 1, :]
        are = cr * pr - ci * pim
        aim = cr * pim + ci * pr
        k_ref[0, j] = (jnp.dot(are, br, preferred_element_type=F32, precision=lax.Precision.HIGHEST)
                       - jnp.dot(aim, bi, preferred_element_type=F32, precision=lax.Precision.HIGHEST))


def _s5_lag_kernels(cr, ci, bbr, bbi, pwr, pwi):
    g, p, n = cr.shape
    ts = pwr.shape[1] - 1
    cspec = pl.BlockSpec((1, p, n), lambda i: (i, 0, 0))
    bspec = pl.BlockSpec((1, n, p), lambda i: (i, 0, 0))
    pspec = pl.BlockSpec((1, ts + 1, n), lambda i: (i, 0, 0))
    return pl.pallas_call(
        _s5_lag_kernel,
        grid=(g,),
        in_specs=[cspec, cspec, bspec, bspec, pspec, pspec],
        out_specs=pl.BlockSpec((1, ts, p, p), lambda i: (i, 0, 0, 0)),
        out_shape=jax.ShapeDtypeStruct((g, ts, p, p), F32),
        compiler_params=_params("parallel"),
        name="s5_lag_kernels",
    )(cr, ci, bbr, bbi, pwr, pwi)


def _s5_operators(a_re, a_im, b_re, b_im, c_re, c_im, log_dt):
    ts, gl = S5_SUB, S5_LANE_GROUPS
    g, n = a_re.shape
    p = b_re.shape[-1]
    nc = g // gl
    dt = jnp.exp(log_dt)[:, None]
    lre, lim = dt * a_re, dt * a_im
    mag = jnp.exp(lre)
    abar_re, abar_im = mag * jnp.cos(lim), mag * jnp.sin(lim)
    den = a_re * a_re + a_im * a_im
    z_re = ((abar_re - 1.0) * a_re + abar_im * a_im) / den
    z_im = (abar_im * a_re - (abar_re - 1.0) * a_im) / den
    bb_re = z_re[..., None] * b_re - z_im[..., None] * b_im
    bb_im = z_re[..., None] * b_im + z_im[..., None] * b_re
    steps = jnp.arange(ts + 1, dtype=F32)[None, :, None]
    pmag = jnp.exp(steps * lre[:, None, :])
    pw_re = pmag * jnp.cos(steps * lim[:, None, :])
    pw_im = pmag * jnp.sin(steps * lim[:, None, :])

    k = _s5_lag_kernels(c_re, c_im, bb_re, bb_im, pw_re, pw_im)
    eye = jnp.eye(gl, dtype=F32)
    s_idx = jnp.arange(ts)[:, None]
    t_idx = jnp.arange(ts)[None, :]
    lag = t_idx - s_idx
    kst = jnp.where((lag >= 0)[None, :, :, None, None], k[:, jnp.clip(lag, 0, ts - 1)], 0.0)
    kst = kst.reshape(nc, gl, ts, ts, p, p)
    m_op = jnp.einsum("cgstpq,gh->csgqthp", kst, eye).reshape(nc, ts * gl * p, ts * gl * p)

    rev = pw_re[:, ts - 1::-1][:, :ts], pw_im[:, ts - 1::-1][:, :ts]
    sb_re = rev[0][..., None] * bb_re[:, None] - rev[1][..., None] * bb_im[:, None]
    sb_im = rev[0][..., None] * bb_im[:, None] + rev[1][..., None] * bb_re[:, None]
    sb = jnp.stack([sb_re, sb_im], axis=0).reshape(2, nc, gl, ts, n, p)
    s_op = jnp.einsum("rcgsnq,gh->csgqrhn", sb, eye).reshape(nc, ts * gl * p, 2 * gl * n)

    ct_re = c_re[:, None] * pw_re[:, 1:, None, :] - c_im[:, None] * pw_im[:, 1:, None, :]
    ct_im = c_re[:, None] * pw_im[:, 1:, None, :] + c_im[:, None] * pw_re[:, 1:, None, :]
    cw = jnp.stack([ct_re, -ct_im], axis=0).reshape(2, nc, gl, ts, p, n)
    c_op = jnp.einsum("rcgtpn,gh->crgnthp", cw, eye).reshape(nc, 2 * gl * n, ts * gl * p)
    return m_op.astype(BF16), s_op.astype(BF16), c_op.astype(BF16), (lre, lim)


def _s5_wpow(lre, lim, n_pow):
    ts, gl = S5_SUB, S5_LANE_GROUPS
    g, n = lre.shape
    nc = g // gl
    e = (ts * (2.0 ** jnp.arange(n_pow, dtype=F32)))[None, :, None]
    mag = jnp.exp(e * lre[:, None, :])
    wr = (mag * jnp.cos(e * lim[:, None, :])).reshape(nc, gl, n_pow, n)
    wi = (mag * jnp.sin(e * lim[:, None, :])).reshape(nc, gl, n_pow, n)
    w = jnp.stack([wr, wi], axis=0)
    return jnp.transpose(w, (1, 3, 0, 2, 4)).reshape(nc, n_pow, 2 * gl * n)


def _s5_kernel(x_ref, m_ref, s_ref, c_ref, w_ref, d_ref, y_ref, *, n_pow):
    x = x_ref[0]
    ds = jnp.dot(x, s_ref[0], preferred_element_type=F32)
    half = ds.shape[1] // 2
    rows = ds.shape[0]
    hr, hi = ds[:, :half], ds[:, half:]
    row = lax.broadcasted_iota(jnp.int32, hr.shape, 0)
    d = 1
    for k in range(n_pow):
        if d >= rows:
            break
        wr, wi = w_ref[0, k:k + 1, :half], w_ref[0, k:k + 1, half:]
        m = row >= d
        sr = jnp.where(m, pltpu.roll(hr, d, 0), 0.0)
        si = jnp.where(m, pltpu.roll(hi, d, 0), 0.0)
        hr, hi = hr + (wr * sr - wi * si), hi + (wr * si + wi * sr)
        d *= 2
    m = row >= 1
    hr = jnp.where(m, pltpu.roll(hr, 1, 0), 0.0)
    hi = jnp.where(m, pltpu.roll(hi, 1, 0), 0.0)
    hs = jnp.concatenate([hr, hi], axis=1).astype(BF16)
    y = jnp.dot(x, m_ref[0], preferred_element_type=F32) + jnp.dot(hs, c_ref[0], preferred_element_type=F32)
    y_ref[0] = _gelu(y + d_ref[0] * x.astype(F32)).astype(y_ref.dtype)


def _s5_branch(u3, m_op, s_op, c_op, wpow, d_skip, bsz):
    ts = S5_SUB
    nc, t, _ = u3.shape
    rows = t // ts // bsz
    n_pow = wpow.shape[1]
    u2 = u3.reshape(nc, t // ts, ts * LANES)
    x_spec = pl.BlockSpec((1, rows, ts * LANES), lambda c, b: (c, b, 0))
    op_spec = lambda a: pl.BlockSpec((1,) + a.shape[1:], lambda c, b: (c, 0, 0), pipeline_mode=pl.Buffered(1))
    d_t = jnp.tile(d_skip.reshape(nc, 1, LANES), (1, 1, ts))
    ys = pl.pallas_call(
        functools.partial(_s5_kernel, n_pow=n_pow),
        grid=(nc, bsz),
        in_specs=[x_spec, op_spec(m_op), op_spec(s_op), op_spec(c_op), op_spec(wpow), op_spec(d_t)],
        out_specs=x_spec,
        out_shape=jax.ShapeDtypeStruct(u2.shape, BF16),
        compiler_params=_params("arbitrary", "arbitrary"),
        name="s5_branch",
    )(u2, m_op, s_op, c_op, wpow, d_t)
    return ys.reshape(nc, t, LANES)


def _route(logits_t, bias):
    e, tm = logits_t.shape
    per = e // N_EXPERT_GROUPS
    scores = _sigmoid(logits_t)
    biased = scores + bias
    g3 = biased.reshape(N_EXPERT_GROUPS, per, tm)
    sub = lax.broadcasted_iota(jnp.int32, g3.shape, 1)
    m1 = jnp.max(g3, axis=1, keepdims=True)
    first = jnp.min(jnp.where(g3 == m1, sub, per), axis=1, keepdims=True)
    m2 = jnp.max(jnp.where(sub == first, -jnp.inf, g3), axis=1, keepdims=True)
    gs = (m1 + m2).reshape(N_EXPERT_GROUPS, tm)
    gidx = lax.broadcasted_iota(jnp.int32, gs.shape, 0)
    grank = jnp.zeros(gs.shape, jnp.int32)
    for j in range(N_EXPERT_GROUPS):
        other = gs[j:j + 1, :]
        ahead = jnp.logical_or(other > gs, jnp.logical_and(other == gs, j < gidx))
        grank = grank + ahead.astype(jnp.int32)
    gsel = (grank < TOPK_GROUPS).reshape(N_EXPERT_GROUPS, 1, tm)
    masked = jnp.where(gsel, g3, -jnp.inf).reshape(e, tm)
    eidx = lax.broadcasted_iota(jnp.int32, masked.shape, 0)
    erank = jnp.zeros(masked.shape, jnp.int32)
    for j in range(e):
        other = masked[j:j + 1, :]
        ahead = jnp.logical_or(other > masked, jnp.logical_and(other == masked, j < eidx))
        erank = erank + ahead.astype(jnp.int32)
    sel = erank < TOP_K
    gate = jnp.where(sel, scores, 0.0)
    denom = jnp.sum(gate, axis=0, keepdims=True)
    return ROUTED_SCALE * gate / denom


def _mixout_kernel(ys_ref, ma_ref, gb_ref, x_ref, wglu_ref, bglu_ref, wout_ref, g_ref, b_ref, rw_ref, rb_ref,
                   x1_ref, comb_ref, *, alpha):
    d = x_ref.shape[1]
    ys = jnp.concatenate([ys_ref[c] for c in range(ys_ref.shape[0])], axis=1)
    glu = jnp.dot(ys, wglu_ref[...], preferred_element_type=F32) + bglu_ref[...]
    yb = glu[:, :d] * _sigmoid(glu[:, d:])
    merged = ma_ref[...].astype(F32) + _sigmoid(gb_ref[...].astype(F32)) * yb
    mix = jnp.dot(merged.astype(BF16), wout_ref[...], preferred_element_type=F32)
    x1 = _layer_norm(alpha * x_ref[...] + mix, g_ref[...], b_ref[...])
    x1_ref[...] = x1
    logits_t = lax.dot_general(rw_ref[...], x1, (((1,), (1,)), ((), ())),
                               preferred_element_type=F32, precision=lax.Precision.HIGHEST)
    comb_ref[...] = _route(logits_t, rb_ref[...]).T


def _mixout(ys, ma, gb, x, wglu, bglu, wout, g, b, rw_t, rb, alpha, tm):
    t, d = x.shape
    e = rw_t.shape[0]
    row = lambda i: (i, 0)
    return pl.pallas_call(
        functools.partial(_mixout_kernel, alpha=alpha),
        grid=(t // tm,),
        in_specs=[pl.BlockSpec((ys.shape[0], tm, LANES), lambda i: (0, i, 0))] +
                 [pl.BlockSpec((tm, d), row)] * 3 +
                 [_const_spec(wglu.shape), _const_spec((1, 2 * d)), _const_spec(wout.shape),
                  _const_spec((1, d)), _const_spec((1, d)), _const_spec(rw_t.shape), _const_spec((e, 1))],
        out_specs=[pl.BlockSpec((tm, d), row), pl.BlockSpec((tm, e), row)],
        out_shape=[jax.ShapeDtypeStruct((t, d), F32), jax.ShapeDtypeStruct((t, e), F32)],
        compiler_params=_params("parallel"),
        name="mixout",
    )(ys, ma, gb, x, wglu, bglu.reshape(1, 2 * d), wout, g.reshape(1, d), b.reshape(1, d), rw_t, rb.reshape(e, 1))


def _moe_kernel(x_ref, comb_ref, wgu_ref, wd_ref, sgu_ref, sd_ref, p_ref, pw_ref, pgw_ref, pgb_ref, g_ref, b_ref,
                o_ref, xb_ref, acc_ref, *, alpha):
    c = pl.program_id(1)
    f = wd_ref.shape[1]

    def ffn(xb, wgu, wd, scale):
        gu = jnp.dot(xb, wgu, preferred_element_type=F32)
        gg = gu[:, :f]
        hdn = gg * _sigmoid(gg) * gu[:, f:]
        if scale is not None:
            hdn = hdn * scale
        return jnp.dot(hdn.astype(BF16), wd, preferred_element_type=F32)

    @pl.when(c == 0)
    def _():
        x = x_ref[...]
        xb = x.astype(BF16)
        xb_ref[...] = xb
        gate = _sigmoid(jnp.dot(xb, pgw_ref[...], preferred_element_type=F32) + pgb_ref[...])
        ple = gate * jnp.dot(p_ref[...].astype(BF16), pw_ref[...], preferred_element_type=F32)
        acc_ref[...] = alpha * x + ple + ffn(xb, sgu_ref[...], sd_ref[...], None)

    xb = xb_ref[...]
    comb = comb_ref[0]
    y = None
    for e in range(wgu_ref.shape[0]):
        ye = ffn(xb, wgu_ref[e], wd_ref[e], comb[:, e:e + 1])
        y = ye if y is None else y + ye
    acc_ref[...] += y

    @pl.when(c == pl.num_programs(1) - 1)
    def _():
        o_ref[...] = _layer_norm(acc_ref[...], g_ref[...], b_ref[...])


def _moe(x1, comb3, wgu, wd, sgu, sd, p, pw, pgw, pgb, g, b, alpha, tm, ec):
    t, d = x1.shape
    e, _, f2 = wgu.shape
    f = f2 // 2
    dp = p.shape[1]
    return pl.pallas_call(
        functools.partial(_moe_kernel, alpha=alpha),
        grid=(t // tm, e // ec),
        in_specs=[pl.BlockSpec((tm, d), lambda i, c: (i, 0)),
                  pl.BlockSpec((1, tm, ec), lambda i, c: (c, i, 0)),
                  pl.BlockSpec((ec, d, f2), lambda i, c: (c, 0, 0)),
                  pl.BlockSpec((ec, f, d), lambda i, c: (c, 0, 0)),
                  _const_spec(sgu.shape), _const_spec(sd.shape),
                  pl.BlockSpec((tm, dp), lambda i, c: (i, 0)),
                  _const_spec(pw.shape), _const_spec(pgw.shape), _const_spec((1, d)),
                  _const_spec((1, d)), _const_spec((1, d))],
        out_specs=pl.BlockSpec((tm, d), lambda i, c: (i, 0)),
        out_shape=jax.ShapeDtypeStruct((t, d), F32),
        scratch_shapes=[pltpu.VMEM((tm, d), BF16), pltpu.VMEM((tm, d), F32)],
        compiler_params=_params("parallel", "arbitrary"),
        name="moe",
    )(x1, comb3, wgu, wd, sgu, sd, p, pw, pgw, pgb.reshape(1, d), g.reshape(1, d), b.reshape(1, d))


def _pick_tile(n, target):
    tm = min(n, target)
    assert n % tm == 0, (n, tm)
    return tm


def kernel(x, p, w_in, conv_w, conv_b, lru_wa, lru_ba, lru_wx, lru_bx, lru_lambda, w_lru_out, s5_a_re, s5_a_im, s5_b_re, s5_b_im, s5_c_re, s5_c_im, s5_d, s5_log_dt, w_glu, b_glu, w_out, ln1_g, ln1_b, router_w, router_bias, moe_w_gate, moe_w_up, moe_w_down, shared_w_gate, shared_w_up, shared_w_down, ple_w, ple_gate_w, ple_gate_b, ln2_g, ln2_b):
    depth = w_in.shape[0]
    bsz, seq, d = x.shape
    t = bsz * seq
    d_rnn = conv_w.shape[-1]
    d_s5 = s5_d.shape[-1]
    n_exp = router_w.shape[-1]
    alpha = (2.0 * depth) ** 0.25
    widths = (d_rnn, d_rnn, d_s5, d, d)
    assert sum(widths) == w_in.shape[-1]
    assert d_s5 % LANES == 0 and seq % S5_SUB == 0

    heads, dh = lru_wa.shape[1], lru_wa.shape[2]
    hpb = (LANES // math.gcd(dh, LANES))
    assert heads % hpb == 0
    ec = 4
    tm_in = _pick_tile(t, 512)
    tm_lru = _pick_tile(seq, 256)
    tm_mix = _pick_tile(t, 512)
    tm_moe = _pick_tile(t, 1024)
    n_pow = max(1, int(math.ceil(math.log2(seq // S5_SUB))))

    h = x.reshape(t, d)
    p2 = p.reshape(depth, t, p.shape[-1])
    for i in range(depth):
        x_lru, g_lru, u_s5, gate_a, gate_b = _inproj(h, w_in[i].astype(BF16), widths,
                                                     (False, False, True, False, False), tm_in)
        m_a = _lru_branch(x_lru, g_lru, gate_a, conv_w[i], conv_b[i],
                          _block_diag_heads(lru_wa[i], hpb), lru_ba[i].reshape(-1),
                          _block_diag_heads(lru_wx[i], hpb), lru_bx[i].reshape(-1),
                          lru_lambda[i], w_lru_out[i].astype(BF16), bsz, tm_lru)
        m_op, s_op, c_op, (lre, lim) = _s5_operators(s5_a_re[i], s5_a_im[i], s5_b_re[i], s5_b_im[i],
                                                    s5_c_re[i], s5_c_im[i], s5_log_dt[i])
        ys = _s5_branch(u_s5, m_op, s_op, c_op, _s5_wpow(lre, lim, n_pow), s5_d[i], bsz)
        x1, comb = _mixout(ys, m_a, gate_b, h, w_glu[i].astype(BF16), b_glu[i], w_out[i].astype(BF16),
                           ln1_g[i], ln1_b[i], router_w[i].T, router_bias[i], alpha, tm_mix)
        comb3 = comb.reshape(t, n_exp // ec, ec).transpose(1, 0, 2)
        wgu = jnp.concatenate([moe_w_gate[i], moe_w_up[i]], axis=-1).astype(BF16)
        sgu = jnp.concatenate([shared_w_gate[i], shared_w_up[i]], axis=-1).astype(BF16)
        h = _moe(x1, comb3, wgu, moe_w_down[i].astype(BF16), sgu, shared_w_down[i].astype(BF16),
                 p2[i], ple_w[i].astype(BF16), ple_gate_w[i].astype(BF16), ple_gate_b[i],
                 ln2_g[i], ln2_b[i], alpha, tm_moe, ec)
    return h.reshape(bsz, seq, d)
```

```python
import functools
import math

import jax
import jax.numpy as jnp
from jax import lax
from jax.experimental import pallas as pl
from jax.experimental.pallas import tpu as pltpu

F32 = jnp.float32
BF16 = jnp.bfloat16

LRU_C = 8.0
TOP_K = 8
N_EXPERT_GROUPS = 8
TOPK_GROUPS = 4
ROUTED_SCALE = 2.5
LN_EPS = 1e-5

LANES = 128
SUBLANES = 8
MXU_DIM = 256
VMEM_LIMIT_BYTES = 56 * 1024 * 1024

S5_SUB = 16
S5_LANE_GROUPS = 8


def _params(*sem):
    return pltpu.CompilerParams(dimension_semantics=sem, vmem_limit_bytes=VMEM_LIMIT_BYTES)


def _const_spec(shape):
    nd = len(shape)
    return pl.BlockSpec(shape, lambda *_: (0,) * nd, pipeline_mode=pl.Buffered(1))


def _gelu(x):
    return 0.5 * x * (1.0 + jnp.tanh(math.sqrt(2.0 / math.pi) * (x + 0.044715 * (x * x * x))))


def _sigmoid(x):
    return 1.0 / (1.0 + jnp.exp(-x))


def _layer_norm(v, g, b):
    mu = jnp.mean(v, axis=-1, keepdims=True)
    c = v - mu
    var = jnp.mean(c * c, axis=-1, keepdims=True)
    return c * lax.rsqrt(var + LN_EPS) * g + b


def _inproj_kernel(x_ref, w_ref, *o_refs, bounds):
    xb = x_ref[...].astype(BF16)
    for o_ref, (lo, hi) in zip(o_refs, bounds):
        z = jnp.dot(xb, w_ref[:, lo:hi], preferred_element_type=F32).astype(o_ref.dtype)
        if len(o_ref.shape) == 3:
            for c in range(o_ref.shape[0]):
                o_ref[c] = z[:, c * LANES:(c + 1) * LANES]
        else:
            o_ref[...] = z


def _inproj(x, w_bf16, widths, tiled, tm):
    t, d = x.shape
    bounds, lo = [], 0
    for w in widths:
        bounds.append((lo, lo + w))
        lo += w
    out_specs, out_shape = [], []
    for w, tl in zip(widths, tiled):
        if tl:
            out_specs.append(pl.BlockSpec((w // LANES, tm, LANES), lambda i: (0, i, 0)))
            out_shape.append(jax.ShapeDtypeStruct((w // LANES, t, LANES), BF16))
        else:
            out_specs.append(pl.BlockSpec((tm, w), lambda i: (i, 0)))
            out_shape.append(jax.ShapeDtypeStruct((t, w), BF16))
    return pl.pallas_call(
        functools.partial(_inproj_kernel, bounds=tuple(bounds)),
        grid=(t // tm,),
        in_specs=[pl.BlockSpec((tm, d), lambda i: (i, 0)), _const_spec(w_bf16.shape)],
        out_specs=out_specs,
        out_shape=out_shape,
        compiler_params=_params("parallel"),
        name="inproj",
    )(x, w_bf16)


def _scan_rows(a, b):
    n = a.shape[0]
    row = lax.broadcasted_iota(jnp.int32, a.shape, 0)
    d = 1
    while d < n:
        a_s = pltpu.roll(a, d, 0)
        b_s = pltpu.roll(b, d, 0)
        m = row >= d
        b = jnp.where(m, a * b_s + b, b)
        a = jnp.where(m, a * a_s, a)
        d *= 2
    return a, b


def _lru_kernel(x_ref, g_ref, ga_ref, cw_ref, cb_ref, wa_ref, ba_ref, wx_ref, bx_ref, lam_ref, wo_ref,
                o_ref, prev_ref, h_ref, *, n_blk, blk):
    i = pl.program_id(1)

    @pl.when(i == 0)
    def _():
        prev_ref[...] = jnp.zeros_like(prev_ref)
        h_ref[...] = jnp.zeros_like(h_ref)

    tm = x_ref.shape[0]
    x = x_ref[...].astype(F32)
    ext = jnp.concatenate([prev_ref[...], x], axis=0)
    prev_ref[...] = x[tm - SUBLANES:, :]
    kw = cw_ref.shape[0]
    xc = cb_ref[...] + cw_ref[kw - 1:kw, :] * x
    for j in range(1, kw):
        xc = xc + cw_ref[kw - 1 - j:kw - j, :] * pltpu.roll(ext, j, 0)[SUBLANES:, :]

    xcb = xc.astype(BF16)
    r_parts, i_parts = [], []
    for k in range(n_blk):
        xk = xcb[:, k * blk:(k + 1) * blk]
        r_parts.append(jnp.dot(xk, wa_ref[k], preferred_element_type=F32))
        i_parts.append(jnp.dot(xk, wx_ref[k], preferred_element_type=F32))
    r = _sigmoid(jnp.concatenate(r_parts, axis=1) + ba_ref[...])
    ig = _sigmoid(jnp.concatenate(i_parts, axis=1) + bx_ref[...])

    nl = -lam_ref[...]
    softplus = jnp.maximum(nl, 0.0) + jnp.log1p(jnp.exp(-jnp.abs(nl)))
    log_a = (-LRU_C) * r * softplus
    a = jnp.exp(log_a)
    mult = jnp.sqrt(1.0 - a * a)
    row = lax.broadcasted_iota(jnp.int32, a.shape, 0)
    mult = jnp.where(jnp.logical_and(row == 0, i == 0), 1.0, mult)
    b = mult * (ig * xc)

    a_cum, h_loc = _scan_rows(a, b)
    h = h_loc + a_cum * h_ref[0:1, :]
    h_ref[...] = jnp.broadcast_to(h[tm - 1:tm, :], h_ref.shape)

    y = (_gelu(g_ref[...].astype(F32)) * h).astype(BF16)
    ya = jnp.dot(y, wo_ref[...], preferred_element_type=F32)
    o_ref[...] = (_sigmoid(ga_ref[...].astype(F32)) * ya).astype(o_ref.dtype)


def _lru_branch(x_lru, g_lru, gate_a, conv_w, conv_b, wa_blk, ba, wx_blk, bx, lam, wo_bf16, bsz, tm):
    t, c = x_lru.shape
    d = gate_a.shape[1]
    seq = t // bsz
    nt = seq // tm
    n_blk, blk, _ = wa_blk.shape
    row = lambda b, i: (b * nt + i, 0)
    return pl.pallas_call(
        functools.partial(_lru_kernel, n_blk=n_blk, blk=blk),
        grid=(bsz, nt),
        in_specs=[pl.BlockSpec((tm, c), row), pl.BlockSpec((tm, c), row), pl.BlockSpec((tm, d), row),
                  _const_spec(conv_w.shape), _const_spec((1, c)),
                  _const_spec(wa_blk.shape), _const_spec((1, c)),
                  _const_spec(wx_blk.shape), _const_spec((1, c)),
                  _const_spec((1, c)), _const_spec(wo_bf16.shape)],
        out_specs=pl.BlockSpec((tm, d), row),
        out_shape=jax.ShapeDtypeStruct((t, d), BF16),
        scratch_shapes=[pltpu.VMEM((SUBLANES, c), F32), pltpu.VMEM((SUBLANES, c), F32)],
        compiler_params=_params("arbitrary", "arbitrary"),
        name="lru_branch",
    )(x_lru, g_lru, gate_a, conv_w, conv_b.reshape(1, c), wa_blk, ba.reshape(1, c),
      wx_blk, bx.reshape(1, c), lam.reshape(1, c), wo_bf16)


def _block_diag_heads(w, heads_per_blk):
    h, dh, _ = w.shape
    nb = h // heads_per_blk
    eye = jnp.eye(heads_per_blk, dtype=w.dtype)
    wb = w.reshape(nb, heads_per_blk, dh, dh)
    out = jnp.einsum("nhij,hk->nhikj", wb, eye)
    return out.reshape(nb, heads_per_blk * dh, heads_per_blk * dh).astype(BF16)


def _s5_lag_kernel(cr_ref, ci_ref, br_ref, bi_ref, pr_ref, pi_ref, k_ref):
    cr, ci = cr_ref[0], ci_ref[0]
    br, bi = br_ref[0], bi_ref[0]
    for j in range(k_ref.shape[1]):
        pr, pim = pr_ref[0, j:j + 1, :], pi_ref[0, j:j + 1, :]
        are = cr * pr - ci * pim
        aim = cr * pim + ci * pr
        k_ref[0, j] = (jnp.dot(are, br, preferred_element_type=F32, precision=lax.Precision.HIGHEST)
                       - jnp.dot(aim, bi, preferred_element_type=F32, precision=lax.Precision.HIGHEST))


def _s5_lag_kernels(cr, ci, bbr, bbi, pwr, pwi):
    g, p, n = cr.shape
    ts = pwr.shape[1] - 1
    cspec = pl.BlockSpec((1, p, n), lambda i: (i, 0, 0))
    bspec = pl.BlockSpec((1, n, p), lambda i: (i, 0, 0))
    pspec = pl.BlockSpec((1, ts + 1, n), lambda i: (i, 0, 0))
    return pl.pallas_call(
        _s5_lag_kernel,
        grid=(g,),
        in_specs=[cspec, cspec, bspec, bspec, pspec, pspec],
        out_specs=pl.BlockSpec((1, ts, p, p), lambda i: (i, 0, 0, 0)),
        out_shape=jax.ShapeDtypeStruct((g, ts, p, p), F32),
        compiler_params=_params("parallel"),
        name="s5_lag_kernels",
    )(cr, ci, bbr, bbi, pwr, pwi)


def _s5_operators(a_re, a_im, b_re, b_im, c_re, c_im, log_dt):
    ts, gl = S5_SUB, S5_LANE_GROUPS
    g, n = a_re.shape
    p = b_re.shape[-1]
    nc = g // gl
    dt = jnp.exp(log_dt)[:, None]
    lre, lim = dt * a_re, dt * a_im
    mag = jnp.exp(lre)
    abar_re, abar_im = mag * jnp.cos(lim), mag * jnp.sin(lim)
    den = a_re * a_re + a_im * a_im
    z_re = ((abar_re - 1.0) * a_re + abar_im * a_im) / den
    z_im = (abar_im * a_re - (abar_re - 1.0) * a_im) / den
    bb_re = z_re[..., None] * b_re - z_im[..., None] * b_im
    bb_im = z_re[..., None] * b_im + z_im[..., None] * b_re
    steps = jnp.arange(ts + 1, dtype=F32)[None, :, None]
    pmag = jnp.exp(steps * lre[:, None, :])
    pw_re = pmag * jnp.cos(steps * lim[:, None, :])
    pw_im = pmag * jnp.sin(steps * lim[:, None, :])

    k = _s5_lag_kernels(c_re, c_im, bb_re, bb_im, pw_re, pw_im)
    kt = jnp.swapaxes(k, 2, 3).reshape(nc, gl, ts, p, p).transpose(0, 2, 1, 3, 4)
    same = (jnp.arange(gl)[:, None, None, None] == jnp.arange(gl)[None, None, :, None])
    bd = jnp.where(same, kt[:, :, :, :, None, :], 0.0).reshape(nc, ts, gl * p, gl * p).astype(BF16)

    rev_re, rev_im = pw_re[:, ts - 1::-1][:, :ts], pw_im[:, ts - 1::-1][:, :ts]
    bt_re, bt_im = jnp.swapaxes(bb_re, 1, 2), jnp.swapaxes(bb_im, 1, 2)
    sb_re = rev_re[:, :, None, :] * bt_re[:, None] - rev_im[:, :, None, :] * bt_im[:, None]
    sb_im = rev_re[:, :, None, :] * bt_im[:, None] + rev_im[:, :, None, :] * bt_re[:, None]

    def rows_sgq(a):
        a = a.reshape(nc, gl, ts, p, n).transpose(0, 2, 1, 3, 4).reshape(nc, ts * gl * p, n)
        return jnp.concatenate([a] * (LANES // n), axis=-1)
    s_re2, s_im2 = rows_sgq(sb_re), rows_sgq(sb_im)

    c4 = lambda a: a.reshape(nc, gl, p, n).transpose(0, 3, 1, 2)[:, :, None]
    p4 = lambda a: a[:, 1:].reshape(nc, gl, ts, n).transpose(0, 3, 2, 1)[..., None]
    ct_re = c4(c_re) * p4(pw_re) - c4(c_im) * p4(pw_im)
    ct_im = c4(c_re) * p4(pw_im) + c4(c_im) * p4(pw_re)
    cwc = jnp.stack([ct_re, -ct_im], axis=1).reshape(nc, 2, n, ts * gl * p)
    return bd, s_re2, s_im2, cwc, (lre, lim)


def _s5_wpow(lre, lim, n_pow):
    ts, gl = S5_SUB, S5_LANE_GROUPS
    g, n = lre.shape
    nc = g // gl
    e = (ts * (2.0 ** jnp.arange(n_pow, dtype=F32)))[None, :, None]
    mag = jnp.exp(e * lre[:, None, :])
    wr = (mag * jnp.cos(e * lim[:, None, :])).reshape(nc, gl, n_pow, n)
    wi = (mag * jnp.sin(e * lim[:, None, :])).reshape(nc, gl, n_pow, n)
    w = jnp.stack([wr, wi], axis=0)
    return jnp.transpose(w, (1, 3, 0, 2, 4)).reshape(nc, n_pow, 2 * gl * n)


def _s5_assemble(bd_ref, sre_ref, sim_ref, cw_ref, m_scr, s_scr, c_scr):
    ts = bd_ref.shape[1]
    n = cw_ref.shape[2]
    half = s_scr.shape[1] // 2
    gl = half // n
    p = LANES // gl
    for s in range(ts):
        for t in range(ts):
            blk = (slice(s * LANES, (s + 1) * LANES), slice(t * LANES, (t + 1) * LANES))
            if t >= s:
                m_scr[blk] = bd_ref[0, t - s]
            elif t * LANES // MXU_DIM == s * LANES // MXU_DIM:
                m_scr[blk] = jnp.zeros((LANES, LANES), m_scr.dtype)
    shape = (s_scr.shape[0], half)
    own = ((lax.broadcasted_iota(jnp.int32, shape, 0) // p) % gl) == (lax.broadcasted_iota(jnp.int32, shape, 1) // n)
    for ref, lo in ((sre_ref, 0), (sim_ref, half)):
        wide = jnp.concatenate([ref[0]] * (half // LANES), axis=1)
        s_scr[:, lo:lo + half] = jnp.where(own, wide, 0.0).astype(s_scr.dtype)
    colg = (lax.broadcasted_iota(jnp.int32, (n, c_scr.shape[1]), 1) // p) % gl
    for r in range(2):
        for g in range(gl):
            c_scr[r * half + g * n:r * half + (g + 1) * n, :] = jnp.where(colg == g, cw_ref[0, r], 0.0).astype(c_scr.dtype)


def _s5_kernel(x_ref, bd_ref, sre_ref, sim_ref, cw_ref, w_ref, d_ref, y_ref, m_scr, s_scr, c_scr, *, n_pow):
    @pl.when(pl.program_id(1) == 0)
    def _():
        _s5_assemble(bd_ref, sre_ref, sim_ref, cw_ref, m_scr, s_scr, c_scr)

    x = x_ref[0]
    ds = jnp.dot(x, s_scr[...], preferred_element_type=F32)
    half = ds.shape[1] // 2
    rows = ds.shape[0]
    hr, hi = ds[:, :half], ds[:, half:]
    row = lax.broadcasted_iota(jnp.int32, hr.shape, 0)
    d = 1
    for k in range(n_pow):
        if d >= rows:
            break
        wr, wi = w_ref[0, k:k + 1, :half], w_ref[0, k:k + 1, half:]
        m = row >= d
        sr = jnp.where(m, pltpu.roll(hr, d, 0), 0.0)
        si = jnp.where(m, pltpu.roll(hi, d, 0), 0.0)
        hr, hi = hr + (wr * sr - wi * si), hi + (wr * si + wi * sr)
        d *= 2
    m = row >= 1
    hr = jnp.where(m, pltpu.roll(hr, 1, 0), 0.0)
    hi = jnp.where(m, pltpu.roll(hi, 1, 0), 0.0)
    hs = jnp.concatenate([hr, hi], axis=1).astype(BF16)
    for j in range(x.shape[1] // MXU_DIM):
        lo, hi_ = j * MXU_DIM, (j + 1) * MXU_DIM
        y = (jnp.dot(x[:, :hi_], m_scr[:hi_, lo:hi_], preferred_element_type=F32)
             + jnp.dot(hs, c_scr[:, lo:hi_], preferred_element_type=F32))
        y = y + d_ref[0, :, lo:hi_] * x[:, lo:hi_].astype(F32)
        y_ref[0, :, lo:hi_] = _gelu(y).astype(y_ref.dtype)


def _s5_branch(u3, bd, s_re2, s_im2, cwc, wpow, d_skip, bsz):
    ts = S5_SUB
    nc, t, _ = u3.shape
    rows = t // ts // bsz
    n_pow = wpow.shape[1]
    width = ts * LANES
    n_state2 = 2 * S5_LANE_GROUPS * cwc.shape[2]
    u2 = u3.reshape(nc, t // ts, width)
    x_spec = pl.BlockSpec((1, rows, width), lambda c, b: (c, b, 0))
    op_spec = lambda a: pl.BlockSpec((1,) + a.shape[1:], lambda c, b: (c,) + (0,) * (a.ndim - 1),
                                     pipeline_mode=pl.Buffered(1))
    d_t = jnp.tile(d_skip.reshape(nc, 1, LANES), (1, 1, ts))
    ys = pl.pallas_call(
        functools.partial(_s5_kernel, n_pow=n_pow),
        grid=(nc, bsz),
        in_specs=[x_spec, op_spec(bd), op_spec(s_re2), op_spec(s_im2), op_spec(cwc), op_spec(wpow), op_spec(d_t)],
        out_specs=x_spec,
        out_shape=jax.ShapeDtypeStruct(u2.shape, BF16),
        scratch_shapes=[pltpu.VMEM((width, width), BF16), pltpu.VMEM((width, n_state2), BF16),
                        pltpu.VMEM((n_state2, width), BF16)],
        compiler_params=_params("arbitrary", "arbitrary"),
        name="s5_branch",
    )(u2, bd, s_re2, s_im2, cwc, wpow, d_t)
    return ys.reshape(nc, t, LANES)


def _route(logits_t, bias):
    e, tm = logits_t.shape
    per = e // N_EXPERT_GROUPS
    scores = _sigmoid(logits_t)
    biased = scores + bias
    g3 = biased.reshape(N_EXPERT_GROUPS, per, tm)
    sub = lax.broadcasted_iota(jnp.int32, g3.shape, 1)
    m1 = jnp.max(g3, axis=1, keepdims=True)
    first = jnp.min(jnp.where(g3 == m1, sub, per), axis=1, keepdims=True)
    m2 = jnp.max(jnp.where(sub == first, -jnp.inf, g3), axis=1, keepdims=True)
    gs = (m1 + m2).reshape(N_EXPERT_GROUPS, tm)
    gidx = lax.broadcasted_iota(jnp.int32, gs.shape, 0)
    grank = jnp.zeros(gs.shape, jnp.int32)
    for j in range(N_EXPERT_GROUPS):
        other = gs[j:j + 1, :]
        ahead = jnp.logical_or(other > gs, jnp.logical_and(other == gs, j < gidx))
        grank = grank + ahead.astype(jnp.int32)
    gsel = (grank < TOPK_GROUPS).reshape(N_EXPERT_GROUPS, 1, tm)
    masked = jnp.where(gsel, g3, -jnp.inf).reshape(e, tm)
    eidx = lax.broadcasted_iota(jnp.int32, masked.shape, 0)
    erank = jnp.zeros(masked.shape, jnp.int32)
    for j in range(e):
        other = masked[j:j + 1, :]
        ahead = jnp.logical_or(other > masked, jnp.logical_and(other == masked, j < eidx))
        erank = erank + ahead.astype(jnp.int32)
    sel = erank < TOP_K
    gate = jnp.where(sel, scores, 0.0)
    denom = jnp.sum(gate, axis=0, keepdims=True)
    return ROUTED_SCALE * gate / denom


def _mixout_kernel(ys_ref, ma_ref, gb_ref, x_ref, wglu_ref, bglu_ref, wout_ref, g_ref, b_ref, rw_ref, rb_ref,
                   x1_ref, comb_ref, *, alpha):
    d = x_ref.shape[1]
    ys = jnp.concatenate([ys_ref[c] for c in range(ys_ref.shape[0])], axis=1)
    glu = jnp.dot(ys, wglu_ref[...], preferred_element_type=F32) + bglu_ref[...]
    yb = glu[:, :d] * _sigmoid(glu[:, d:])
    merged = ma_ref[...].astype(F32) + _sigmoid(gb_ref[...].astype(F32)) * yb
    mix = jnp.dot(merged.astype(BF16), wout_ref[...], preferred_element_type=F32)
    x1 = _layer_norm(alpha * x_ref[...] + mix, g_ref[...], b_ref[...])
    x1_ref[...] = x1
    logits_t = lax.dot_general(rw_ref[...], x1, (((1,), (1,)), ((), ())),
                               preferred_element_type=F32, precision=lax.Precision.HIGHEST)
    comb_ref[...] = _route(logits_t, rb_ref[...]).T


def _mixout(ys, ma, gb, x, wglu, bglu, wout, g, b, rw_t, rb, alpha, tm):
    t, d = x.shape
    e = rw_t.shape[0]
    row = lambda i: (i, 0)
    return pl.pallas_call(
        functools.partial(_mixout_kernel, alpha=alpha),
        grid=(t // tm,),
        in_specs=[pl.BlockSpec((ys.shape[0], tm, LANES), lambda i: (0, i, 0))] +
                 [pl.BlockSpec((tm, d), row)] * 3 +
                 [_const_spec(wglu.shape), _const_spec((1, 2 * d)), _const_spec(wout.shape),
                  _const_spec((1, d)), _const_spec((1, d)), _const_spec(rw_t.shape), _const_spec((e, 1))],
        out_specs=[pl.BlockSpec((tm, d), row), pl.BlockSpec((tm, e), row)],
        out_shape=[jax.ShapeDtypeStruct((t, d), F32), jax.ShapeDtypeStruct((t, e), F32)],
        compiler_params=_params("parallel"),
        name="mixout",
    )(ys, ma, gb, x, wglu, bglu.reshape(1, 2 * d), wout, g.reshape(1, d), b.reshape(1, d), rw_t, rb.reshape(e, 1))


def _moe_kernel(x_ref, comb_ref, wgu_ref, wd_ref, sgu_ref, sd_ref, p_ref, pw_ref, pgw_ref, pgb_ref, g_ref, b_ref,
                o_ref, xb_ref, acc_ref, *, alpha):
    c = pl.program_id(1)
    f = wd_ref.shape[1]

    def ffn(xb, wgu, wd, scale):
        gu = jnp.dot(xb, wgu, preferred_element_type=F32)
        gg = gu[:, :f]
        hdn = gg * _sigmoid(gg) * gu[:, f:]
        if scale is not None:
            hdn = hdn * scale
        return jnp.dot(hdn.astype(BF16), wd, preferred_element_type=F32)

    @pl.when(c == 0)
    def _():
        x = x_ref[...]
        xb = x.astype(BF16)
        xb_ref[...] = xb
        gate = _sigmoid(jnp.dot(xb, pgw_ref[...], preferred_element_type=F32) + pgb_ref[...])
        ple = gate * jnp.dot(p_ref[...].astype(BF16), pw_ref[...], preferred_element_type=F32)
        acc_ref[...] = alpha * x + ple + ffn(xb, sgu_ref[...], sd_ref[...], None)

    xb = xb_ref[...]
    comb = comb_ref[0]
    y = None
    for e in range(wgu_ref.shape[0]):
        ye = ffn(xb, wgu_ref[e], wd_ref[e], comb[:, e:e + 1])
        y = ye if y is None else y + ye
    acc_ref[...] += y

    @pl.when(c == pl.num_programs(1) - 1)
    def _():
        o_ref[...] = _layer_norm(acc_ref[...], g_ref[...], b_ref[...])


def _moe(x1, comb3, wgu, wd, sgu, sd, p, pw, pgw, pgb, g, b, alpha, tm, ec):
    t, d = x1.shape
    e, _, f2 = wgu.shape
    f = f2 // 2
    dp = p.shape[1]
    return pl.pallas_call(
        functools.partial(_moe_kernel, alpha=alpha),
        grid=(t // tm, e // ec),
        in_specs=[pl.BlockSpec((tm, d), lambda i, c: (i, 0)),
                  pl.BlockSpec((1, tm, ec), lambda i, c: (c, i, 0)),
                  pl.BlockSpec((ec, d, f2), lambda i, c: (c, 0, 0)),
                  pl.BlockSpec((ec, f, d), lambda i, c: (c, 0, 0)),
                  _const_spec(sgu.shape), _const_spec(sd.shape),
                  pl.BlockSpec((tm, dp), lambda i, c: (i, 0)),
                  _const_spec(pw.shape), _const_spec(pgw.shape), _const_spec((1, d)),
                  _const_spec((1, d)), _const_spec((1, d))],
        out_specs=pl.BlockSpec((tm, d), lambda i, c: (i, 0)),
        out_shape=jax.ShapeDtypeStruct((t, d), F32),
        scratch_shapes=[pltpu.VMEM((tm, d), BF16), pltpu.VMEM((tm, d), F32)],
        compiler_params=_params("parallel", "arbitrary"),
        name="moe",
    )(x1, comb3, wgu, wd, sgu, sd, p, pw, pgw, pgb.reshape(1, d), g.reshape(1, d), b.reshape(1, d))


def _pick_tile(n, target):
    tm = min(n, target)
    assert n % tm == 0, (n, tm)
    return tm


def kernel(x, p, w_in, conv_w, conv_b, lru_wa, lru_ba, lru_wx, lru_bx, lru_lambda, w_lru_out, s5_a_re, s5_a_im, s5_b_re, s5_b_im, s5_c_re, s5_c_im, s5_d, s5_log_dt, w_glu, b_glu, w_out, ln1_g, ln1_b, router_w, router_bias, moe_w_gate, moe_w_up, moe_w_down, shared_w_gate, shared_w_up, shared_w_down, ple_w, ple_gate_w, ple_gate_b, ln2_g, ln2_b):
    depth = w_in.shape[0]
    bsz, seq, d = x.shape
    t = bsz * seq
    d_rnn = conv_w.shape[-1]
    d_s5 = s5_d.shape[-1]
    n_exp = router_w.shape[-1]
    alpha = (2.0 * depth) ** 0.25
    widths = (d_rnn, d_rnn, d_s5, d, d)
    assert sum(widths) == w_in.shape[-1]
    assert d_s5 % LANES == 0 and seq % S5_SUB == 0

    heads, dh = lru_wa.shape[1], lru_wa.shape[2]
    hpb = (LANES // math.gcd(dh, LANES))
    assert heads % hpb == 0
    ec = 4
    tm_in = _pick_tile(t, 512)
    tm_lru = _pick_tile(seq, 256)
    tm_mix = _pick_tile(t, 512)
    tm_moe = _pick_tile(t, 1024)
    n_pow = max(1, int(math.ceil(math.log2(seq // S5_SUB))))

    h = x.reshape(t, d)
    p2 = p.reshape(depth, t, p.shape[-1])
    for i in range(depth):
        x_lru, g_lru, u_s5, gate_a, gate_b = _inproj(h, w_in[i].astype(BF16), widths,
                                                     (False, False, True, False, False), tm_in)
        m_a = _lru_branch(x_lru, g_lru, gate_a, conv_w[i], conv_b[i],
                          _block_diag_heads(lru_wa[i], hpb), lru_ba[i].reshape(-1),
                          _block_diag_heads(lru_wx[i], hpb), lru_bx[i].reshape(-1),
                          lru_lambda[i], w_lru_out[i].astype(BF16), bsz, tm_lru)
        bd, s_re2, s_im2, cwc, (lre, lim) = _s5_operators(s5_a_re[i], s5_a_im[i], s5_b_re[i], s5_b_im[i],
                                                         s5_c_re[i], s5_c_im[i], s5_log_dt[i])
        ys = _s5_branch(u_s5, bd, s_re2, s_im2, cwc, _s5_wpow(lre, lim, n_pow), s5_d[i], bsz)
        x1, comb = _mixout(ys, m_a, gate_b, h, w_glu[i].astype(BF16), b_glu[i], w_out[i].astype(BF16),
                           ln1_g[i], ln1_b[i], router_w[i].T, router_bias[i], alpha, tm_mix)
        comb3 = comb.reshape(t, n_exp // ec, ec).transpose(1, 0, 2)
        wgu = jnp.concatenate([moe_w_gate[i], moe_w_up[i]], axis=-1).astype(BF16)
        sgu = jnp.concatenate([shared_w_gate[i], shared_w_up[i]], axis=-1).astype(BF16)
        h = _moe(x1, comb3, wgu, moe_w_down[i].astype(BF16), sgu, shared_w_down[i].astype(BF16),
                 p2[i], ple_w[i].astype(BF16), ple_gate_w[i].astype(BF16), ple_gate_b[i],
                 ln2_g[i], ln2_b[i], alpha, tm_moe, ec)
    return h.reshape(bsz, seq, d)
```

```python
import functools
import math

import jax
import jax.numpy as jnp
from jax import lax
from jax.experimental import pallas as pl
from jax.experimental.pallas import tpu as pltpu

F32 = jnp.float32
BF16 = jnp.bfloat16

LRU_C = 8.0
TOP_K = 8
N_EXPERT_GROUPS = 8
TOPK_GROUPS = 4
ROUTED_SCALE = 2.5
LN_EPS = 1e-5

LANES = 128
SUBLANES = 8
MXU_DIM = 256
VMEM_LIMIT_BYTES = 56 * 1024 * 1024

S5_SUB = 16
S5_LANE_GROUPS = 8


def _params(*sem):
    return pltpu.CompilerParams(dimension_semantics=sem, vmem_limit_bytes=VMEM_LIMIT_BYTES)


def _const_spec(shape):
    nd = len(shape)
    return pl.BlockSpec(shape, lambda *_: (0,) * nd, pipeline_mode=pl.Buffered(1))


def _gelu(x):
    return 0.5 * x * (1.0 + jnp.tanh(math.sqrt(2.0 / math.pi) * (x + 0.044715 * (x * x * x))))


def _sigmoid(x):
    return 1.0 / (1.0 + jnp.exp(-x))


def _layer_norm(v, g, b):
    mu = jnp.mean(v, axis=-1, keepdims=True)
    c = v - mu
    var = jnp.mean(c * c, axis=-1, keepdims=True)
    return c * lax.rsqrt(var + LN_EPS) * g + b


def _inproj_kernel(x_ref, w_ref, *refs, bounds):
    o_refs, z_scr = refs[:-1], refs[-1]
    xb = x_ref[...].astype(BF16)
    for o_ref, (lo, hi) in zip(o_refs, bounds):
        z = jnp.dot(xb, w_ref[:, lo:hi], preferred_element_type=F32)
        if len(o_ref.shape) == 3:
            ts = o_ref.shape[2] // LANES
            for c in range(o_ref.shape[0]):
                z_scr[c] = z[:, c * LANES:(c + 1) * LANES]
                for s in range(ts):
                    o_ref[c, :, s * LANES:(s + 1) * LANES] = z_scr[
                        c, pl.ds(s, o_ref.shape[1], stride=ts), :].astype(o_ref.dtype)
        else:
            o_ref[...] = z.astype(o_ref.dtype)


def _inproj(x, w_bf16, widths, flat_ts, tm):
    t, d = x.shape
    bounds, lo = [], 0
    for w in widths:
        bounds.append((lo, lo + w))
        lo += w
    out_specs, out_shape, scr_w = [], [], LANES
    for w, ts in zip(widths, flat_ts):
        if ts:
            out_specs.append(pl.BlockSpec((w // LANES, tm // ts, ts * LANES), lambda i: (0, i, 0)))
            out_shape.append(jax.ShapeDtypeStruct((w // LANES, t // ts, ts * LANES), BF16))
            scr_w = max(scr_w, w)
        else:
            out_specs.append(pl.BlockSpec((tm, w), lambda i: (i, 0)))
            out_shape.append(jax.ShapeDtypeStruct((t, w), BF16))
    return pl.pallas_call(
        functools.partial(_inproj_kernel, bounds=tuple(bounds)),
        grid=(t // tm,),
        in_specs=[pl.BlockSpec((tm, d), lambda i: (i, 0)), _const_spec(w_bf16.shape)],
        out_specs=out_specs,
        out_shape=out_shape,
        scratch_shapes=[pltpu.VMEM((scr_w // LANES, tm, LANES), F32)],
        compiler_params=_params("parallel"),
        name="inproj",
    )(x, w_bf16)


def _scan_rows(a, b, h0):
    n, c = a.shape
    a = a.reshape(n // SUBLANES, SUBLANES, c)
    b = b.reshape(n // SUBLANES, SUBLANES, c)
    sub = lax.broadcasted_iota(jnp.int32, a.shape, 1)
    d = 1
    while d < SUBLANES:
        m = sub >= d
        b = jnp.where(m, a * pltpu.roll(b, d, 1) + b, b)
        a = jnp.where(m, a * pltpu.roll(a, d, 1), a)
        d *= 2
    out, carry = [], h0
    for g in range(n // SUBLANES):
        hg = b[g] + a[g] * carry
        out.append(hg)
        carry = hg[SUBLANES - 1:SUBLANES]
    return jnp.concatenate(out, axis=0)


def _lru_kernel(x_ref, g_ref, ga_ref, cw_ref, cb_ref, wa_ref, ba_ref, wx_ref, bx_ref, lam_ref, wo_ref,
                o_ref, prev_ref, h_ref, *, n_blk, blk):
    i = pl.program_id(1)

    @pl.when(i == 0)
    def _():
        prev_ref[...] = jnp.zeros_like(prev_ref)
        h_ref[...] = jnp.zeros_like(h_ref)

    tm = x_ref.shape[0]
    x = x_ref[...].astype(F32)
    ext = jnp.concatenate([prev_ref[...], x], axis=0)
    prev_ref[...] = x[tm - SUBLANES:, :]
    kw = cw_ref.shape[0]
    xc = cb_ref[...] + cw_ref[kw - 1:kw, :] * x
    for j in range(1, kw):
        xc = xc + cw_ref[kw - 1 - j:kw - j, :] * pltpu.roll(ext, j, 0)[SUBLANES:, :]

    xcb = xc.astype(BF16)
    r_parts, i_parts = [], []
    for k in range(n_blk):
        xk = xcb[:, k * blk:(k + 1) * blk]
        r_parts.append(jnp.dot(xk, wa_ref[k], preferred_element_type=F32))
        i_parts.append(jnp.dot(xk, wx_ref[k], preferred_element_type=F32))
    r = _sigmoid(jnp.concatenate(r_parts, axis=1) + ba_ref[...])
    ig = _sigmoid(jnp.concatenate(i_parts, axis=1) + bx_ref[...])

    nl = -lam_ref[...]
    softplus = jnp.maximum(nl, 0.0) + jnp.log1p(jnp.exp(-jnp.abs(nl)))
    log_a = (-LRU_C) * r * softplus
    a = jnp.exp(log_a)
    v = 1.0 - a * a
    mult = jnp.where(v > 0.0, v * lax.rsqrt(v), 0.0)
    row = lax.broadcasted_iota(jnp.int32, a.shape, 0)
    mult = jnp.where(jnp.logical_and(row == 0, i == 0), 1.0, mult)
    b = mult * (ig * xc)

    h = _scan_rows(a, b, h_ref[0:1, :])
    h_ref[...] = jnp.broadcast_to(h[tm - 1:tm, :], h_ref.shape)

    y = (_gelu(g_ref[...].astype(F32)) * h).astype(BF16)
    ya = jnp.dot(y, wo_ref[...], preferred_element_type=F32)
    o_ref[...] = (_sigmoid(ga_ref[...].astype(F32)) * ya).astype(o_ref.dtype)


def _lru_branch(x_lru, g_lru, gate_a, conv_w, conv_b, wa_blk, ba, wx_blk, bx, lam, wo_bf16, bsz, tm):
    t, c = x_lru.shape
    d = gate_a.shape[1]
    seq = t // bsz
    nt = seq // tm
    n_blk, blk, _ = wa_blk.shape
    row = lambda b, i: (b * nt + i, 0)
    return pl.pallas_call(
        functools.partial(_lru_kernel, n_blk=n_blk, blk=blk),
        grid=(bsz, nt),
        in_specs=[pl.BlockSpec((tm, c), row), pl.BlockSpec((tm, c), row), pl.BlockSpec((tm, d), row),
                  _const_spec(conv_w.shape), _const_spec((1, c)),
                  _const_spec(wa_blk.shape), _const_spec((1, c)),
                  _const_spec(wx_blk.shape), _const_spec((1, c)),
                  _const_spec((1, c)), _const_spec(wo_bf16.shape)],
        out_specs=pl.BlockSpec((tm, d), row),
        out_shape=jax.ShapeDtypeStruct((t, d), BF16),
        scratch_shapes=[pltpu.VMEM((SUBLANES, c), F32), pltpu.VMEM((SUBLANES, c), F32)],
        compiler_params=_params("arbitrary", "arbitrary"),
        name="lru_branch",
    )(x_lru, g_lru, gate_a, conv_w, conv_b.reshape(1, c), wa_blk, ba.reshape(1, c),
      wx_blk, bx.reshape(1, c), lam.reshape(1, c), wo_bf16)


def _block_diag_heads(w, heads_per_blk):
    h, dh, _ = w.shape
    nb = h // heads_per_blk
    eye = jnp.eye(heads_per_blk, dtype=w.dtype)
    wb = w.reshape(nb, heads_per_blk, dh, dh)
    out = jnp.einsum("nhij,hk->nhikj", wb, eye)
    return out.reshape(nb, heads_per_blk * dh, heads_per_blk * dh).astype(BF16)


def _s5_lag_kernel(cr_ref, ci_ref, br_ref, bi_ref, pr_ref, pi_ref, k_ref):
    gb, p, n = cr_ref.shape
    ts = k_ref.shape[1]
    for g in range(gb):
        cr, ci = cr_ref[g][None], ci_ref[g][None]
        pr, pim = pr_ref[g, :ts], pi_ref[g, :ts]
        are = (cr * pr - ci * pim).reshape(ts * p, n)
        aim = (cr * pim + ci * pr).reshape(ts * p, n)
        k = (jnp.dot(are, br_ref[g], preferred_element_type=F32, precision=lax.Precision.HIGHEST)
             - jnp.dot(aim, bi_ref[g], preferred_element_type=F32, precision=lax.Precision.HIGHEST))
        k_ref[g] = k.reshape(ts, p, p)


def _s5_lag_kernels(cr, ci, bbr, bbi, pwr, pwi):
    g, p, n = cr.shape
    ts = pwr.shape[1] - 1
    gb = S5_LANE_GROUPS
    cspec = pl.BlockSpec((gb, p, n), lambda i: (i, 0, 0))
    bspec = pl.BlockSpec((gb, n, p), lambda i: (i, 0, 0))
    pspec = pl.BlockSpec((gb, ts + 1, 1, n), lambda i: (i, 0, 0, 0))
    return pl.pallas_call(
        _s5_lag_kernel,
        grid=(g // gb,),
        in_specs=[cspec, cspec, bspec, bspec, pspec, pspec],
        out_specs=pl.BlockSpec((gb, ts, p, p), lambda i: (i, 0, 0, 0)),
        out_shape=jax.ShapeDtypeStruct((g, ts, p, p), F32),
        compiler_params=_params("parallel"),
        name="s5_lag_kernels",
    )(cr, ci, bbr, bbi, pwr[:, :, None, :], pwi[:, :, None, :])


def _s5_operators(a_re, a_im, b_re, b_im, c_re, c_im, log_dt):
    ts, gl = S5_SUB, S5_LANE_GROUPS
    g, n = a_re.shape
    p = b_re.shape[-1]
    nc = g // gl
    dt = jnp.exp(log_dt)[:, None]
    lre, lim = dt * a_re, dt * a_im
    mag = jnp.exp(lre)
    abar_re, abar_im = mag * jnp.cos(lim), mag * jnp.sin(lim)
    den = a_re * a_re + a_im * a_im
    z_re = ((abar_re - 1.0) * a_re + abar_im * a_im) / den
    z_im = (abar_im * a_re - (abar_re - 1.0) * a_im) / den
    bb_re = z_re[..., None] * b_re - z_im[..., None] * b_im
    bb_im = z_re[..., None] * b_im + z_im[..., None] * b_re
    steps = jnp.arange(ts + 1, dtype=F32)[None, :, None]
    pmag = jnp.exp(steps * lre[:, None, :])
    pw_re = pmag * jnp.cos(steps * lim[:, None, :])
    pw_im = pmag * jnp.sin(steps * lim[:, None, :])

    k = _s5_lag_kernels(c_re, c_im, bb_re, bb_im, pw_re, pw_im)
    kt = jnp.swapaxes(k, 2, 3).reshape(nc, gl, ts, p, p).transpose(0, 2, 1, 3, 4)
    same = (jnp.arange(gl)[:, None, None, None] == jnp.arange(gl)[None, None, :, None])
    bd = jnp.where(same, kt[:, :, :, :, None, :], 0.0).reshape(nc, ts, gl * p, gl * p).astype(BF16)

    rev_re, rev_im = pw_re[:, ts - 1::-1][:, :ts], pw_im[:, ts - 1::-1][:, :ts]
    bt_re, bt_im = jnp.swapaxes(bb_re, 1, 2), jnp.swapaxes(bb_im, 1, 2)
    sb_re = rev_re[:, :, None, :] * bt_re[:, None] - rev_im[:, :, None, :] * bt_im[:, None]
    sb_im = rev_re[:, :, None, :] * bt_im[:, None] + rev_im[:, :, None, :] * bt_re[:, None]

    def rows_sgq(a):
        a = a.reshape(nc, gl, ts, p, n).transpose(0, 2, 1, 3, 4).reshape(nc, ts * gl * p, n)
        return jnp.concatenate([a] * (LANES // n), axis=-1)
    s_re2, s_im2 = rows_sgq(sb_re), rows_sgq(sb_im)

    c4 = lambda a: a.reshape(nc, gl, p, n).transpose(0, 3, 1, 2)[:, :, None]
    p4 = lambda a: a[:, 1:].reshape(nc, gl, ts, n).transpose(0, 3, 2, 1)[..., None]
    ct_re = c4(c_re) * p4(pw_re) - c4(c_im) * p4(pw_im)
    ct_im = c4(c_re) * p4(pw_im) + c4(c_im) * p4(pw_re)
    cwc = jnp.stack([ct_re, -ct_im], axis=1).reshape(nc, 2, n, ts * gl * p)
    return bd, s_re2, s_im2, cwc, (lre, lim)


def _s5_wpow(lre, lim, n_pow):
    ts, gl = S5_SUB, S5_LANE_GROUPS
    g, n = lre.shape
    nc = g // gl
    e = (ts * (2.0 ** jnp.arange(n_pow, dtype=F32)))[None, :, None]
    mag = jnp.exp(e * lre[:, None, :])
    wr = (mag * jnp.cos(e * lim[:, None, :])).reshape(nc, gl, n_pow, n)
    wi = (mag * jnp.sin(e * lim[:, None, :])).reshape(nc, gl, n_pow, n)
    w = jnp.stack([wr, wi], axis=0)
    return jnp.transpose(w, (1, 3, 0, 2, 4)).reshape(nc, n_pow, 2 * gl * n)


def _s5_assemble(bd_ref, sre_ref, sim_ref, cw_ref, m_scr, s_scr, c_scr):
    ts = bd_ref.shape[1]
    n = cw_ref.shape[2]
    half = s_scr.shape[1] // 2
    gl = half // n
    p = LANES // gl
    for s in range(ts):
        for t in range(ts):
            blk = (slice(s * LANES, (s + 1) * LANES), slice(t * LANES, (t + 1) * LANES))
            if t >= s:
                m_scr[blk] = bd_ref[0, t - s]
            elif t * LANES // MXU_DIM == s * LANES // MXU_DIM:
                m_scr[blk] = jnp.zeros((LANES, LANES), m_scr.dtype)
    shape = (s_scr.shape[0], half)
    own = ((lax.broadcasted_iota(jnp.int32, shape, 0) // p) % gl) == (lax.broadcasted_iota(jnp.int32, shape, 1) // n)
    for ref, lo in ((sre_ref, 0), (sim_ref, half)):
        wide = jnp.concatenate([ref[0]] * (half // LANES), axis=1)
        s_scr[:, lo:lo + half] = jnp.where(own, wide, 0.0).astype(s_scr.dtype)
    colg = (lax.broadcasted_iota(jnp.int32, (n, c_scr.shape[1]), 1) // p) % gl
    for r in range(2):
        for g in range(gl):
            c_scr[r * half + g * n:r * half + (g + 1) * n, :] = jnp.where(colg == g, cw_ref[0, r], 0.0).astype(c_scr.dtype)


def _s5_kernel(x_ref, bd_ref, sre_ref, sim_ref, cw_ref, w_ref, d_ref, y_ref, m_scr, s_scr, c_scr, *, n_pow):
    @pl.when(pl.program_id(1) == 0)
    def _():
        _s5_assemble(bd_ref, sre_ref, sim_ref, cw_ref, m_scr, s_scr, c_scr)

    x = x_ref[0]
    ds = jnp.dot(x, s_scr[...], preferred_element_type=F32)
    half = ds.shape[1] // 2
    rows = ds.shape[0]
    hr, hi = ds[:, :half], ds[:, half:]
    row = lax.broadcasted_iota(jnp.int32, hr.shape, 0)
    d = 1
    for k in range(n_pow):
        if d >= rows:
            break
        wr, wi = w_ref[0, k:k + 1, :half], w_ref[0, k:k + 1, half:]
        m = row >= d
        sr = jnp.where(m, pltpu.roll(hr, d, 0), 0.0)
        si = jnp.where(m, pltpu.roll(hi, d, 0), 0.0)
        hr, hi = hr + (wr * sr - wi * si), hi + (wr * si + wi * sr)
        d *= 2
    m = row >= 1
    hr = jnp.where(m, pltpu.roll(hr, 1, 0), 0.0)
    hi = jnp.where(m, pltpu.roll(hi, 1, 0), 0.0)
    hs = jnp.concatenate([hr, hi], axis=1).astype(BF16)
    for j in range(x.shape[1] // MXU_DIM):
        lo, hi_ = j * MXU_DIM, (j + 1) * MXU_DIM
        y = (jnp.dot(x[:, :hi_], m_scr[:hi_, lo:hi_], preferred_element_type=F32)
             + jnp.dot(hs, c_scr[:, lo:hi_], preferred_element_type=F32))
        y = y + d_ref[0, :, lo:hi_] * x[:, lo:hi_].astype(F32)
        y_ref[0, :, lo:hi_] = _gelu(y).astype(y_ref.dtype)


def _s5_branch(u2, bd, s_re2, s_im2, cwc, wpow, d_skip, bsz):
    ts = S5_SUB
    nc, n_sub, width = u2.shape
    rows = n_sub // bsz
    n_pow = wpow.shape[1]
    n_state2 = 2 * S5_LANE_GROUPS * cwc.shape[2]
    x_spec = pl.BlockSpec((1, rows, width), lambda c, b: (c, b, 0))
    op_spec = lambda a: pl.BlockSpec((1,) + a.shape[1:], lambda c, b: (c,) + (0,) * (a.ndim - 1),
                                     pipeline_mode=pl.Buffered(1))
    d_t = jnp.tile(d_skip.reshape(nc, 1, LANES), (1, 1, ts))
    ys = pl.pallas_call(
        functools.partial(_s5_kernel, n_pow=n_pow),
        grid=(nc, bsz),
        in_specs=[x_spec, op_spec(bd), op_spec(s_re2), op_spec(s_im2), op_spec(cwc), op_spec(wpow), op_spec(d_t)],
        out_specs=x_spec,
        out_shape=jax.ShapeDtypeStruct(u2.shape, BF16),
        scratch_shapes=[pltpu.VMEM((width, width), BF16), pltpu.VMEM((width, n_state2), BF16),
                        pltpu.VMEM((n_state2, width), BF16)],
        compiler_params=_params("arbitrary", "arbitrary"),
        name="s5_branch",
    )(u2, bd, s_re2, s_im2, cwc, wpow, d_t)
    return ys


def _route(logits_t, bias):
    e, tm = logits_t.shape
    per = e // N_EXPERT_GROUPS
    scores = _sigmoid(logits_t)
    biased = scores + bias
    g3 = biased.reshape(N_EXPERT_GROUPS, per, tm)
    sub = lax.broadcasted_iota(jnp.int32, g3.shape, 1)
    m1 = jnp.max(g3, axis=1, keepdims=True)
    first = jnp.min(jnp.where(g3 == m1, sub, per), axis=1, keepdims=True)
    m2 = jnp.max(jnp.where(sub == first, -jnp.inf, g3), axis=1, keepdims=True)
    gs = (m1 + m2).reshape(N_EXPERT_GROUPS, tm)
    gidx = lax.broadcasted_iota(jnp.int32, gs.shape, 0)
    grank = jnp.zeros(gs.shape, jnp.int32)
    for j in range(N_EXPERT_GROUPS):
        other = gs[j:j + 1, :]
        ahead = jnp.logical_or(other > gs, jnp.logical_and(other == gs, j < gidx))
        grank = grank + ahead.astype(jnp.int32)
    gsel = (grank < TOPK_GROUPS).reshape(N_EXPERT_GROUPS, 1, tm)
    masked = jnp.where(gsel, g3, -jnp.inf).reshape(e, tm)
    eidx = lax.broadcasted_iota(jnp.int32, masked.shape, 0)
    erank = jnp.zeros(masked.shape, jnp.int32)
    for j in range(e):
        other = masked[j:j + 1, :]
        ahead = jnp.logical_or(other > masked, jnp.logical_and(other == masked, j < eidx))
        erank = erank + ahead.astype(jnp.int32)
    sel = erank < TOP_K
    gate = jnp.where(sel, scores, 0.0)
    denom = jnp.sum(gate, axis=0, keepdims=True)
    return ROUTED_SCALE * gate / denom


def _mixout_kernel(ys_ref, ma_ref, gb_ref, x_ref, wglu_ref, bglu_ref, wout_ref, g_ref, b_ref, rw_ref, rb_ref,
                   x1_ref, comb_ref, ys_scr, *, alpha):
    d = x_ref.shape[1]
    nc, n_sub, width = ys_ref.shape
    ts = width // LANES
    for c in range(nc):
        for s in range(ts):
            ys_scr[c, pl.ds(s, n_sub, stride=ts), :] = ys_ref[c, :, s * LANES:(s + 1) * LANES].astype(F32)
    ys = jnp.concatenate([ys_scr[c] for c in range(nc)], axis=1).astype(BF16)
    glu = jnp.dot(ys, wglu_ref[...], preferred_element_type=F32) + bglu_ref[...]
    yb = glu[:, :d] * _sigmoid(glu[:, d:])
    merged = ma_ref[...].astype(F32) + _sigmoid(gb_ref[...].astype(F32)) * yb
    mix = jnp.dot(merged.astype(BF16), wout_ref[...], preferred_element_type=F32)
    x1 = _layer_norm(alpha * x_ref[...] + mix, g_ref[...], b_ref[...])
    x1_ref[...] = x1
    rw = rw_ref[...]
    rw_hi = rw.astype(BF16)
    rw_lo = (rw - rw_hi.astype(F32)).astype(BF16)
    x_hi = x1.astype(BF16)
    x_lo = (x1 - x_hi.astype(F32)).astype(BF16)
    nt = (((1,), (1,)), ((), ()))
    logits_t = (lax.dot_general(rw_hi, x_hi, nt, preferred_element_type=F32)
                + lax.dot_general(rw_hi, x_lo, nt, preferred_element_type=F32)
                + lax.dot_general(rw_lo, x_hi, nt, preferred_element_type=F32))
    comb_ref[...] = _route(logits_t, rb_ref[...]).T


def _mixout(ys, ma, gb, x, wglu, bglu, wout, g, b, rw_t, rb, alpha, tm):
    t, d = x.shape
    e = rw_t.shape[0]
    nc, _, width = ys.shape
    ts = width // LANES
    row = lambda i: (i, 0)
    return pl.pallas_call(
        functools.partial(_mixout_kernel, alpha=alpha),
        grid=(t // tm,),
        in_specs=[pl.BlockSpec((nc, tm // ts, width), lambda i: (0, i, 0))] +
                 [pl.BlockSpec((tm, d), row)] * 3 +
                 [_const_spec(wglu.shape), _const_spec((1, 2 * d)), _const_spec(wout.shape),
                  _const_spec((1, d)), _const_spec((1, d)), _const_spec(rw_t.shape), _const_spec((e, 1))],
        out_specs=[pl.BlockSpec((tm, d), row), pl.BlockSpec((tm, e), row)],
        out_shape=[jax.ShapeDtypeStruct((t, d), F32), jax.ShapeDtypeStruct((t, e), F32)],
        scratch_shapes=[pltpu.VMEM((nc, tm, LANES), F32)],
        compiler_params=_params("parallel"),
        name="mixout",
    )(ys, ma, gb, x, wglu, bglu.reshape(1, 2 * d), wout, g.reshape(1, d), b.reshape(1, d), rw_t, rb.reshape(e, 1))


def _moe_kernel(x_ref, comb_ref, wg_ref, wu_ref, wd_ref, sg_ref, su_ref, sd_ref, p_ref, pw_ref, pgw_ref, pgb_ref,
                g_ref, b_ref, o_ref, xb_ref, acc_ref, *, alpha):
    c = pl.program_id(1)
    ec = wg_ref.shape[0]

    def ffn(xb, wg, wu, wd, scale):
        gg = jnp.dot(xb, wg, preferred_element_type=F32)
        hdn = gg * _sigmoid(gg) * jnp.dot(xb, wu, preferred_element_type=F32)
        if scale is not None:
            hdn = hdn * scale
        return jnp.dot(hdn.astype(BF16), wd, preferred_element_type=F32)

    @pl.when(c == 0)
    def _():
        x = x_ref[...]
        xb = x.astype(BF16)
        xb_ref[...] = xb
        gate = _sigmoid(jnp.dot(xb, pgw_ref[...], preferred_element_type=F32) + pgb_ref[...])
        ple = gate * jnp.dot(p_ref[...].astype(BF16), pw_ref[...], preferred_element_type=F32)
        acc_ref[...] = alpha * x + ple + ffn(xb, sg_ref[...], su_ref[...], sd_ref[...], None)

    xb = xb_ref[...]
    comb = comb_ref[...]
    lane = lax.broadcasted_iota(jnp.int32, comb.shape, 1)
    y = None
    for e in range(ec):
        col = jnp.sum(jnp.where(lane == c * ec + e, comb, 0.0), axis=1, keepdims=True)
        ye = ffn(xb, wg_ref[e], wu_ref[e], wd_ref[e], col)
        y = ye if y is None else y + ye
    acc_ref[...] += y

    @pl.when(c == pl.num_programs(1) - 1)
    def _():
        o_ref[...] = _layer_norm(acc_ref[...], g_ref[...], b_ref[...])


def _moe(x1, comb, wg, wu, wd, sg, su, sd, p, pw, pgw, pgb, g, b, alpha, tm, ec):
    t, d = x1.shape
    e, _, f = wg.shape
    dp = p.shape[1]
    return pl.pallas_call(
        functools.partial(_moe_kernel, alpha=alpha),
        grid=(t // tm, e // ec),
        in_specs=[pl.BlockSpec((tm, d), lambda i, c: (i, 0)),
                  pl.BlockSpec((tm, e), lambda i, c: (i, 0)),
                  pl.BlockSpec((ec, d, f), lambda i, c: (c, 0, 0)),
                  pl.BlockSpec((ec, d, f), lambda i, c: (c, 0, 0)),
                  pl.BlockSpec((ec, f, d), lambda i, c: (c, 0, 0)),
                  _const_spec(sg.shape), _const_spec(su.shape), _const_spec(sd.shape),
                  pl.BlockSpec((tm, dp), lambda i, c: (i, 0)),
                  _const_spec(pw.shape), _const_spec(pgw.shape), _const_spec((1, d)),
                  _const_spec((1, d)), _const_spec((1, d))],
        out_specs=pl.BlockSpec((tm, d), lambda i, c: (i, 0)),
        out_shape=jax.ShapeDtypeStruct((t, d), F32),
        scratch_shapes=[pltpu.VMEM((tm, d), BF16), pltpu.VMEM((tm, d), F32)],
        compiler_params=_params("parallel", "arbitrary"),
        name="moe",
    )(x1, comb, wg, wu, wd, sg, su, sd, p, pw, pgw, pgb.reshape(1, d), g.reshape(1, d), b.reshape(1, d))


def _pick_tile(n, target):
    tm = min(n, target)
    assert n % tm == 0, (n, tm)
    return tm


def kernel(x, p, w_in, conv_w, conv_b, lru_wa, lru_ba, lru_wx, lru_bx, lru_lambda, w_lru_out, s5_a_re, s5_a_im, s5_b_re, s5_b_im, s5_c_re, s5_c_im, s5_d, s5_log_dt, w_glu, b_glu, w_out, ln1_g, ln1_b, router_w, router_bias, moe_w_gate, moe_w_up, moe_w_down, shared_w_gate, shared_w_up, shared_w_down, ple_w, ple_gate_w, ple_gate_b, ln2_g, ln2_b):
    depth = w_in.shape[0]
    bsz, seq, d = x.shape
    t = bsz * seq
    d_rnn = conv_w.shape[-1]
    d_s5 = s5_d.shape[-1]
    alpha = (2.0 * depth) ** 0.25
    widths = (d_rnn, d_rnn, d_s5, d, d)
    assert sum(widths) == w_in.shape[-1]
    assert d_s5 % LANES == 0 and seq % S5_SUB == 0

    heads, dh = lru_wa.shape[1], lru_wa.shape[2]
    hpb = (LANES // math.gcd(dh, LANES))
    assert heads % hpb == 0
    ec = 4
    tm_in = _pick_tile(t, 512)
    tm_lru = _pick_tile(seq, 256)
    tm_mix = _pick_tile(t, 512)
    tm_moe = _pick_tile(t, 1024)
    n_pow = max(1, int(math.ceil(math.log2(seq // S5_SUB))))

    h = x.reshape(t, d)
    p2 = p.reshape(depth, t, p.shape[-1])
    for i in range(depth):
        x_lru, g_lru, u_s5, gate_a, gate_b = _inproj(h, w_in[i].astype(BF16), widths,
                                                     (0, 0, S5_SUB, 0, 0), tm_in)
        m_a = _lru_branch(x_lru, g_lru, gate_a, conv_w[i], conv_b[i],
                          _block_diag_heads(lru_wa[i], hpb), lru_ba[i].reshape(-1),
                          _block_diag_heads(lru_wx[i], hpb), lru_bx[i].reshape(-1),
                          lru_lambda[i], w_lru_out[i].astype(BF16), bsz, tm_lru)
        bd, s_re2, s_im2, cwc, (lre, lim) = _s5_operators(s5_a_re[i], s5_a_im[i], s5_b_re[i], s5_b_im[i],
                                                         s5_c_re[i], s5_c_im[i], s5_log_dt[i])
        ys = _s5_branch(u_s5, bd, s_re2, s_im2, cwc, _s5_wpow(lre, lim, n_pow), s5_d[i], bsz)
        x1, comb = _mixout(ys, m_a, gate_b, h, w_glu[i].astype(BF16), b_glu[i], w_out[i].astype(BF16),
                           ln1_g[i], ln1_b[i], router_w[i].T, router_bias[i], alpha, tm_mix)
        h = _moe(x1, comb, moe_w_gate[i].astype(BF16), moe_w_up[i].astype(BF16), moe_w_down[i].astype(BF16),
                 shared_w_gate[i].astype(BF16), shared_w_up[i].astype(BF16), shared_w_down[i].astype(BF16),
                 p2[i], ple_w[i].astype(BF16), ple_gate_w[i].astype(BF16), ple_gate_b[i],
                 ln2_g[i], ln2_b[i], alpha, tm_moe, ec)
    return h.reshape(bsz, seq, d)
```

```python
import functools
import math

import jax
import jax.numpy as jnp
from jax import lax
from jax.experimental import pallas as pl
from jax.experimental.pallas import tpu as pltpu

F32 = jnp.float32
BF16 = jnp.bfloat16

LRU_C = 8.0
TOP_K = 8
N_EXPERT_GROUPS = 8
TOPK_GROUPS = 4
ROUTED_SCALE = 2.5
LN_EPS = 1e-5

LANES = 128
SUBLANES = 8
MXU_DIM = 256
VMEM_LIMIT_BYTES = 60 * 1024 * 1024

S5_SUB = 16
S5_LANE_GROUPS = 8


def _params(*sem):
    return pltpu.CompilerParams(dimension_semantics=sem, vmem_limit_bytes=VMEM_LIMIT_BYTES)


def _layer_spec(stacked, layer):
    nd = stacked.ndim - 1
    return pl.BlockSpec((None,) + stacked.shape[1:], lambda *_: (layer,) + (0,) * nd, pipeline_mode=pl.Buffered(1))


def _row3(v):
    return v.reshape(v.shape[0], 1, -1)


def _gelu(x):
    return 0.5 * x * (1.0 + jnp.tanh(math.sqrt(2.0 / math.pi) * (x + 0.044715 * (x * x * x))))


def _sigmoid(x):
    return 1.0 / (1.0 + jnp.exp(-x))


def _layer_norm(v, g, b):
    mu = jnp.mean(v, axis=-1, keepdims=True)
    c = v - mu
    var = jnp.mean(c * c, axis=-1, keepdims=True)
    return c * lax.rsqrt(var + LN_EPS) * g + b


def _inproj_kernel(x_ref, w_ref, *refs, bounds):
    o_refs, z_scr = refs[:-1], refs[-1]
    xb = x_ref[...].astype(BF16)
    for o_ref, (lo, hi) in zip(o_refs, bounds):
        z = jnp.dot(xb, w_ref[:, lo:hi], preferred_element_type=F32)
        if len(o_ref.shape) == 3:
            ts = o_ref.shape[2] // LANES
            for c in range(o_ref.shape[0]):
                z_scr[c] = z[:, c * LANES:(c + 1) * LANES]
                for s in range(ts):
                    o_ref[c, :, s * LANES:(s + 1) * LANES] = z_scr[
                        c, pl.ds(s, o_ref.shape[1], stride=ts), :].astype(o_ref.dtype)
        else:
            o_ref[...] = z.astype(o_ref.dtype)


def _inproj(x, w_all, layer, widths, flat_ts, tm):
    t, d = x.shape
    bounds, lo = [], 0
    for w in widths:
        bounds.append((lo, lo + w))
        lo += w
    out_specs, out_shape, scr_w = [], [], LANES
    for w, ts in zip(widths, flat_ts):
        if ts:
            out_specs.append(pl.BlockSpec((w // LANES, tm // ts, ts * LANES), lambda i: (0, i, 0)))
            out_shape.append(jax.ShapeDtypeStruct((w // LANES, t // ts, ts * LANES), BF16))
            scr_w = max(scr_w, w)
        else:
            out_specs.append(pl.BlockSpec((tm, w), lambda i: (i, 0)))
            out_shape.append(jax.ShapeDtypeStruct((t, w), BF16))
    return pl.pallas_call(
        functools.partial(_inproj_kernel, bounds=tuple(bounds)),
        grid=(t // tm,),
        in_specs=[pl.BlockSpec((tm, d), lambda i: (i, 0)), _layer_spec(w_all, layer)],
        out_specs=out_specs,
        out_shape=out_shape,
        scratch_shapes=[pltpu.VMEM((scr_w // LANES, tm, LANES), F32)],
        compiler_params=_params("parallel"),
        name="inproj",
    )(x, w_all)


def _shift_rows(v, d, fill):
    n = v.shape[0]
    if d % SUBLANES == 0:
        head = jnp.broadcast_to(jnp.asarray(fill, v.dtype), (d,) + v.shape[1:])
        return jnp.concatenate([head, v[:n - d]], axis=0)
    row = lax.broadcasted_iota(jnp.int32, v.shape, 0)
    return jnp.where(row >= d, pltpu.roll(v, d, 0), fill)


def _scan_rows(a, b, h0, a_scr, b_scr, h_scr):
    n, c = a.shape
    ng = n // SUBLANES
    for ct in range(c // LANES):
        a_scr[ct] = a[:, ct * LANES:(ct + 1) * LANES]
        b_scr[ct] = b[:, ct * LANES:(ct + 1) * LANES]
    for ct in range(c // LANES):
        every = lambda s: pl.ds(s, ng, stride=SUBLANES)
        a_s = [a_scr[ct, every(s), :] for s in range(SUBLANES)]
        acum, hloc = [a_s[0]], [b_scr[ct, every(0), :]]
        for s in range(1, SUBLANES):
            hloc.append(a_s[s] * hloc[-1] + b_scr[ct, every(s), :])
            acum.append(a_s[s] * acum[-1])
        ga, gh = acum[-1], hloc[-1]
        d = 1
        while d < ng:
            gh = gh + ga * _shift_rows(gh, d, 0.0)
            ga = ga * _shift_rows(ga, d, 1.0)
            d *= 2
        h0c = h0[:, ct * LANES:(ct + 1) * LANES]
        after = gh + ga * h0c
        cin = _shift_rows(after, 1, h0c)
        for s in range(SUBLANES):
            h_scr[ct, every(s), :] = hloc[s] + acum[s] * cin
    return jnp.concatenate([h_scr[ct] for ct in range(c // LANES)], axis=1)


def _lru_kernel(x_ref, g_ref, ga_ref, cw_ref, cb_ref, wa_ref, ba_ref, wx_ref, bx_ref, lam_ref, wo_ref,
                o_ref, prev_ref, h_ref, a_scr, b_scr, hs_scr, *, n_blk, blk):
    i = pl.program_id(1)

    @pl.when(i == 0)
    def _():
        prev_ref[...] = jnp.zeros_like(prev_ref)
        h_ref[...] = jnp.zeros_like(h_ref)

    tm = x_ref.shape[0]
    x = x_ref[...].astype(F32)
    ext = jnp.concatenate([prev_ref[...], x], axis=0)
    prev_ref[...] = x[tm - SUBLANES:, :]
    kw = cw_ref.shape[0]
    xc = cb_ref[...] + cw_ref[kw - 1:kw, :] * x
    for j in range(1, kw):
        xc = xc + cw_ref[kw - 1 - j:kw - j, :] * pltpu.roll(ext, j, 0)[SUBLANES:, :]

    xcb = xc.astype(BF16)
    r_parts, i_parts = [], []
    for k in range(n_blk):
        xk = xcb[:, k * blk:(k + 1) * blk]
        r_parts.append(jnp.dot(xk, wa_ref[k], preferred_element_type=F32))
        i_parts.append(jnp.dot(xk, wx_ref[k], preferred_element_type=F32))
    r = _sigmoid(jnp.concatenate(r_parts, axis=1) + ba_ref[...])
    ig = _sigmoid(jnp.concatenate(i_parts, axis=1) + bx_ref[...])

    nl = -lam_ref[...]
    softplus = jnp.maximum(nl, 0.0) + jnp.log1p(jnp.exp(-jnp.abs(nl)))
    log_a = (-LRU_C) * r * softplus
    a = jnp.exp(log_a)
    v = 1.0 - a * a
    mult = jnp.where(v > 0.0, v * lax.rsqrt(v), 0.0)
    row = lax.broadcasted_iota(jnp.int32, a.shape, 0)
    mult = jnp.where(jnp.logical_and(row == 0, i == 0), 1.0, mult)
    b = mult * (ig * xc)

    h = _scan_rows(a, b, h_ref[0:1, :], a_scr, b_scr, hs_scr)
    h_ref[...] = jnp.broadcast_to(h[tm - 1:tm, :], h_ref.shape)

    y = (_gelu(g_ref[...].astype(F32)) * h).astype(BF16)
    ya = jnp.dot(y, wo_ref[...], preferred_element_type=F32)
    o_ref[...] = (_sigmoid(ga_ref[...].astype(F32)) * ya).astype(o_ref.dtype)


def _lru_branch(x_lru, g_lru, gate_a, layer, conv_w, conv_b, wa_blk, ba, wx_blk, bx, lam, wo_all, bsz, tm):
    t, c = x_lru.shape
    d = gate_a.shape[1]
    seq = t // bsz
    nt = seq // tm
    _, n_blk, blk, _ = wa_blk.shape
    row = lambda b, i: (b * nt + i, 0)
    params = (conv_w, conv_b, wa_blk, ba, wx_blk, bx, lam, wo_all)
    return pl.pallas_call(
        functools.partial(_lru_kernel, n_blk=n_blk, blk=blk),
        grid=(bsz, nt),
        in_specs=[pl.BlockSpec((tm, c), row), pl.BlockSpec((tm, c), row), pl.BlockSpec((tm, d), row)]
                 + [_layer_spec(a, layer) for a in params],
        out_specs=pl.BlockSpec((tm, d), row),
        out_shape=jax.ShapeDtypeStruct((t, d), BF16),
        scratch_shapes=[pltpu.VMEM((SUBLANES, c), F32), pltpu.VMEM((SUBLANES, c), F32)]
                       + [pltpu.VMEM((c // LANES, tm, LANES), F32)] * 3,
        compiler_params=_params("arbitrary", "arbitrary"),
        name="lru_branch",
    )(x_lru, g_lru, gate_a, *params)


def _block_diag_heads(w, heads_per_blk):
    l, h, dh, _ = w.shape
    nb = h // heads_per_blk
    eye = jnp.eye(heads_per_blk, dtype=w.dtype)
    wb = w.reshape(l * nb, heads_per_blk, dh, dh)
    out = jnp.einsum("nhij,hk->nhikj", wb, eye)
    return out.reshape(l, nb, heads_per_blk * dh, heads_per_blk * dh).astype(BF16)


def _s5_lag_kernel(cr_ref, ci_ref, br_ref, bi_ref, pr_ref, pi_ref, k_ref):
    gb, p, n = cr_ref.shape
    ts = k_ref.shape[1]
    for g in range(gb):
        cr, ci = cr_ref[g][None], ci_ref[g][None]
        pr, pim = pr_ref[g, :ts], pi_ref[g, :ts]
        are = (cr * pr - ci * pim).reshape(ts * p, n)
        aim = (cr * pim + ci * pr).reshape(ts * p, n)
        k = (jnp.dot(are, br_ref[g], preferred_element_type=F32, precision=lax.Precision.HIGHEST)
             - jnp.dot(aim, bi_ref[g], preferred_element_type=F32, precision=lax.Precision.HIGHEST))
        k_ref[g] = k.reshape(ts, p, p)


def _s5_lag_kernels(cr, ci, bbr, bbi, pwr, pwi):
    g, p, n = cr.shape
    ts = pwr.shape[1] - 1
    gb = S5_LANE_GROUPS
    cspec = pl.BlockSpec((gb, p, n), lambda i: (i, 0, 0))
    bspec = pl.BlockSpec((gb, n, p), lambda i: (i, 0, 0))
    pspec = pl.BlockSpec((gb, ts + 1, 1, n), lambda i: (i, 0, 0, 0))
    return pl.pallas_call(
        _s5_lag_kernel,
        grid=(g // gb,),
        in_specs=[cspec, cspec, bspec, bspec, pspec, pspec],
        out_specs=pl.BlockSpec((gb, ts, p, p), lambda i: (i, 0, 0, 0)),
        out_shape=jax.ShapeDtypeStruct((g, ts, p, p), F32),
        compiler_params=_params("parallel"),
        name="s5_lag_kernels",
    )(cr, ci, bbr, bbi, pwr[:, :, None, :], pwi[:, :, None, :])


def _s5_operators(a_re, a_im, b_re, b_im, c_re, c_im, log_dt):
    ts, gl = S5_SUB, S5_LANE_GROUPS
    g, n = a_re.shape
    p = b_re.shape[-1]
    nc = g // gl
    dt = jnp.exp(log_dt)[:, None]
    lre, lim = dt * a_re, dt * a_im
    mag = jnp.exp(lre)
    abar_re, abar_im = mag * jnp.cos(lim), mag * jnp.sin(lim)
    den = a_re * a_re + a_im * a_im
    z_re = ((abar_re - 1.0) * a_re + abar_im * a_im) / den
    z_im = (abar_im * a_re - (abar_re - 1.0) * a_im) / den
    bb_re = z_re[..., None] * b_re - z_im[..., None] * b_im
    bb_im = z_re[..., None] * b_im + z_im[..., None] * b_re
    steps = jnp.arange(ts + 1, dtype=F32)[None, :, None]
    pmag = jnp.exp(steps * lre[:, None, :])
    pw_re = pmag * jnp.cos(steps * lim[:, None, :])
    pw_im = pmag * jnp.sin(steps * lim[:, None, :])

    k = _s5_lag_kernels(c_re, c_im, bb_re, bb_im, pw_re, pw_im)
    kt = jnp.swapaxes(k, 2, 3).reshape(nc, gl, ts, p, p).transpose(0, 2, 1, 3, 4)
    same = (jnp.arange(gl)[:, None, None, None] == jnp.arange(gl)[None, None, :, None])
    bd = jnp.where(same, kt[:, :, :, :, None, :], 0.0).reshape(nc, ts, gl * p, gl * p).astype(BF16)

    rev_re, rev_im = pw_re[:, ts - 1::-1][:, :ts], pw_im[:, ts - 1::-1][:, :ts]
    bt_re, bt_im = jnp.swapaxes(bb_re, 1, 2), jnp.swapaxes(bb_im, 1, 2)
    sb_re = rev_re[:, :, None, :] * bt_re[:, None] - rev_im[:, :, None, :] * bt_im[:, None]
    sb_im = rev_re[:, :, None, :] * bt_im[:, None] + rev_im[:, :, None, :] * bt_re[:, None]

    def rows_sgq(a):
        a = a.reshape(nc, gl, ts, p, n).transpose(0, 2, 1, 3, 4).reshape(nc, ts * gl * p, n)
        return jnp.concatenate([a] * (LANES // n), axis=-1)
    s_re2, s_im2 = rows_sgq(sb_re), rows_sgq(sb_im)

    c4 = lambda a: a.reshape(nc, gl, p, n).transpose(0, 3, 1, 2)[:, :, None]
    p4 = lambda a: a[:, 1:].reshape(nc, gl, ts, n).transpose(0, 3, 2, 1)[..., None]
    ct_re = c4(c_re) * p4(pw_re) - c4(c_im) * p4(pw_im)
    ct_im = c4(c_re) * p4(pw_im) + c4(c_im) * p4(pw_re)
    cwc = jnp.stack([ct_re, -ct_im], axis=1).reshape(nc, 2, n, ts * gl * p)
    return bd, s_re2, s_im2, cwc, (lre, lim)


def _s5_wpow(lre, lim, n_pow):
    ts, gl = S5_SUB, S5_LANE_GROUPS
    g, n = lre.shape
    nc = g // gl
    e = (ts * (2.0 ** jnp.arange(n_pow, dtype=F32)))[None, :, None]
    mag = jnp.exp(e * lre[:, None, :])
    wr = (mag * jnp.cos(e * lim[:, None, :])).reshape(nc, gl, n_pow, n)
    wi = (mag * jnp.sin(e * lim[:, None, :])).reshape(nc, gl, n_pow, n)
    w = jnp.stack([wr, wi], axis=0)
    return jnp.transpose(w, (1, 3, 0, 2, 4)).reshape(nc, n_pow, 2 * gl * n)


def _s5_assemble(bd_ref, sre_ref, sim_ref, cw_ref, m_scr, s_scr, c_scr):
    ts = bd_ref.shape[1]
    n = cw_ref.shape[2]
    half = s_scr.shape[1] // 2
    gl = half // n
    p = LANES // gl
    for s in range(ts):
        for t in range(ts):
            blk = (slice(s * LANES, (s + 1) * LANES), slice(t * LANES, (t + 1) * LANES))
            if t >= s:
                m_scr[blk] = bd_ref[0, t - s]
            elif t * LANES // MXU_DIM == s * LANES // MXU_DIM:
                m_scr[blk] = jnp.zeros((LANES, LANES), m_scr.dtype)
    shape = (s_scr.shape[0], half)
    own = ((lax.broadcasted_iota(jnp.int32, shape, 0) // p) % gl) == (lax.broadcasted_iota(jnp.int32, shape, 1) // n)
    for ref, lo in ((sre_ref, 0), (sim_ref, half)):
        wide = jnp.concatenate([ref[0]] * (half // LANES), axis=1)
        s_scr[:, lo:lo + half] = jnp.where(own, wide, 0.0).astype(s_scr.dtype)
    colg = (lax.broadcasted_iota(jnp.int32, (n, c_scr.shape[1]), 1) // p) % gl
    for r in range(2):
        for g in range(gl):
            c_scr[r * half + g * n:r * half + (g + 1) * n, :] = jnp.where(colg == g, cw_ref[0, r], 0.0).astype(c_scr.dtype)


def _s5_kernel(x_ref, bd_ref, sre_ref, sim_ref, cw_ref, w_ref, d_ref, y_ref, m_scr, s_scr, c_scr, *, n_pow):
    @pl.when(pl.program_id(1) == 0)
    def _():
        _s5_assemble(bd_ref, sre_ref, sim_ref, cw_ref, m_scr, s_scr, c_scr)

    x = x_ref[0]
    ds = jnp.dot(x, s_scr[...], preferred_element_type=F32)
    half = ds.shape[1] // 2
    rows = ds.shape[0]
    hr, hi = ds[:, :half], ds[:, half:]
    row = lax.broadcasted_iota(jnp.int32, hr.shape, 0)
    d = 1
    for k in range(n_pow):
        if d >= rows:
            break
        wr, wi = w_ref[0, k:k + 1, :half], w_ref[0, k:k + 1, half:]
        if d % SUBLANES == 0:
            sr, si = hr[:rows - d], hi[:rows - d]
            hr = jnp.concatenate([hr[:d], hr[d:] + (wr * sr - wi * si)], axis=0)
            hi = jnp.concatenate([hi[:d], hi[d:] + (wr * si + wi * sr)], axis=0)
        else:
            m = row >= d
            sr = jnp.where(m, pltpu.roll(hr, d, 0), 0.0)
            si = jnp.where(m, pltpu.roll(hi, d, 0), 0.0)
            hr, hi = hr + (wr * sr - wi * si), hi + (wr * si + wi * sr)
        d *= 2
    m = row >= 1
    hr = jnp.where(m, pltpu.roll(hr, 1, 0), 0.0)
    hi = jnp.where(m, pltpu.roll(hi, 1, 0), 0.0)
    hs = jnp.concatenate([hr, hi], axis=1).astype(BF16)
    for j in range(x.shape[1] // MXU_DIM):
        lo, hi_ = j * MXU_DIM, (j + 1) * MXU_DIM
        y = (jnp.dot(x[:, :hi_], m_scr[:hi_, lo:hi_], preferred_element_type=F32)
             + jnp.dot(hs, c_scr[:, lo:hi_], preferred_element_type=F32))
        y = y + d_ref[0, :, lo:hi_] * x[:, lo:hi_].astype(F32)
        y_ref[0, :, lo:hi_] = _gelu(y).astype(y_ref.dtype)


def _s5_branch(u2, layer, ops, bsz):
    nc, n_sub, width = u2.shape
    rows = n_sub // bsz
    cwc, wpow = ops[3], ops[4]
    n_state2 = 2 * S5_LANE_GROUPS * cwc.shape[3]
    x_spec = pl.BlockSpec((1, rows, width), lambda c, b: (c, b, 0))
    op_spec = lambda a: pl.BlockSpec((None, 1) + a.shape[2:], lambda c, b: (layer, c) + (0,) * (a.ndim - 2),
                                     pipeline_mode=pl.Buffered(1))
    return pl.pallas_call(
        functools.partial(_s5_kernel, n_pow=wpow.shape[2]),
        grid=(nc, bsz),
        in_specs=[x_spec] + [op_spec(a) for a in ops],
        out_specs=x_spec,
        out_shape=jax.ShapeDtypeStruct(u2.shape, BF16),
        scratch_shapes=[pltpu.VMEM((width, width), BF16), pltpu.VMEM((width, n_state2), BF16),
                        pltpu.VMEM((n_state2, width), BF16)],
        compiler_params=_params("arbitrary", "arbitrary"),
        name="s5_branch",
    )(u2, *ops)


def _route(logits_t, bias):
    e, tm = logits_t.shape
    per = e // N_EXPERT_GROUPS
    scores = _sigmoid(logits_t)
    biased = scores + bias
    g3 = biased.reshape(N_EXPERT_GROUPS, per, tm)
    sub = lax.broadcasted_iota(jnp.int32, g3.shape, 1)
    m1 = jnp.max(g3, axis=1, keepdims=True)
    first = jnp.min(jnp.where(g3 == m1, sub, per), axis=1, keepdims=True)
    m2 = jnp.max(jnp.where(sub == first, -jnp.inf, g3), axis=1, keepdims=True)
    gs = (m1 + m2).reshape(N_EXPERT_GROUPS, tm)
    gidx = lax.broadcasted_iota(jnp.int32, gs.shape, 0)
    grank = jnp.zeros(gs.shape, jnp.int32)
    for j in range(N_EXPERT_GROUPS):
        other = gs[j:j + 1, :]
        ahead = jnp.logical_or(other > gs, jnp.logical_and(other == gs, j < gidx))
        grank = grank + ahead.astype(jnp.int32)
    gsel = (grank < TOPK_GROUPS).reshape(N_EXPERT_GROUPS, 1, tm)
    masked = jnp.where(gsel, g3, -jnp.inf).reshape(e, tm)
    eidx = lax.broadcasted_iota(jnp.int32, masked.shape, 0)
    erank = jnp.zeros(masked.shape, jnp.int32)
    for j in range(e):
        other = masked[j:j + 1, :]
        ahead = jnp.logical_or(other > masked, jnp.logical_and(other == masked, j < eidx))
        erank = erank + ahead.astype(jnp.int32)
    sel = erank < TOP_K
    gate = jnp.where(sel, scores, 0.0)
    denom = jnp.sum(gate, axis=0, keepdims=True)
    return ROUTED_SCALE * gate / denom


def _mixout_kernel(ys_ref, ma_ref, gb_ref, x_ref, wglu_ref, bglu_ref, wout_ref, g_ref, b_ref, rw_ref, rb_ref,
                   x1_ref, comb_ref, ys_scr, *, alpha):
    d = x_ref.shape[1]
    nc, n_sub, width = ys_ref.shape
    ts = width // LANES
    for c in range(nc):
        for s in range(ts):
            ys_scr[c, pl.ds(s, n_sub, stride=ts), :] = ys_ref[c, :, s * LANES:(s + 1) * LANES].astype(F32)
    ys = jnp.concatenate([ys_scr[c] for c in range(nc)], axis=1).astype(BF16)
    glu = jnp.dot(ys, wglu_ref[...], preferred_element_type=F32) + bglu_ref[...]
    yb = glu[:, :d] * _sigmoid(glu[:, d:])
    merged = ma_ref[...].astype(F32) + _sigmoid(gb_ref[...].astype(F32)) * yb
    mix = jnp.dot(merged.astype(BF16), wout_ref[...], preferred_element_type=F32)
    x1 = _layer_norm(alpha * x_ref[...] + mix, g_ref[...], b_ref[...])
    x1_ref[...] = x1
    rw = rw_ref[...]
    rw_hi = rw.astype(BF16)
    rw_lo = (rw - rw_hi.astype(F32)).astype(BF16)
    x_hi = x1.astype(BF16)
    x_lo = (x1 - x_hi.astype(F32)).astype(BF16)
    nt = (((1,), (1,)), ((), ()))
    logits_t = (lax.dot_general(rw_hi, x_hi, nt, preferred_element_type=F32)
                + lax.dot_general(rw_hi, x_lo, nt, preferred_element_type=F32)
                + lax.dot_general(rw_lo, x_hi, nt, preferred_element_type=F32))
    comb_ref[...] = _route(logits_t, rb_ref[...]).T


def _mixout(ys, ma, gb, x, layer, params, alpha, tm):
    t, d = x.shape
    e = params[5].shape[1]
    nc, _, width = ys.shape
    ts = width // LANES
    row = lambda i: (i, 0)
    return pl.pallas_call(
        functools.partial(_mixout_kernel, alpha=alpha),
        grid=(t // tm,),
        in_specs=[pl.BlockSpec((nc, tm // ts, width), lambda i: (0, i, 0))] +
                 [pl.BlockSpec((tm, d), row)] * 3 + [_layer_spec(a, layer) for a in params],
        out_specs=[pl.BlockSpec((tm, d), row), pl.BlockSpec((tm, e), row)],
        out_shape=[jax.ShapeDtypeStruct((t, d), F32), jax.ShapeDtypeStruct((t, e), F32)],
        scratch_shapes=[pltpu.VMEM((nc, tm, LANES), F32)],
        compiler_params=_params("parallel"),
        name="mixout",
    )(ys, ma, gb, x, *params)


def _moe_kernel(x_ref, comb_ref, wg_ref, wu_ref, wd_ref, sg_ref, su_ref, sd_ref, p_ref, pw_ref, pgw_ref, pgb_ref,
                g_ref, b_ref, o_ref, xb_ref, acc_ref, *, alpha):
    c = pl.program_id(1)
    ec = wg_ref.shape[0]

    def ffn(xb, wg, wu, wd, scale):
        gg = jnp.dot(xb, wg, preferred_element_type=F32)
        hdn = gg * _sigmoid(gg) * jnp.dot(xb, wu, preferred_element_type=F32)
        if scale is not None:
            hdn = hdn * scale
        return jnp.dot(hdn.astype(BF16), wd, preferred_element_type=F32)

    @pl.when(c == 0)
    def _():
        x = x_ref[...]
        xb = x.astype(BF16)
        xb_ref[...] = xb
        gate = _sigmoid(jnp.dot(xb, pgw_ref[...], preferred_element_type=F32) + pgb_ref[...])
        ple = gate * jnp.dot(p_ref[...].astype(BF16), pw_ref[...], preferred_element_type=F32)
        acc_ref[...] = alpha * x + ple + ffn(xb, sg_ref[...], su_ref[...], sd_ref[...], None)

    xb = xb_ref[...]
    comb = comb_ref[...]
    lane = lax.broadcasted_iota(jnp.int32, comb.shape, 1)
    y = None
    for e in range(ec):
        col = jnp.sum(jnp.where(lane == c * ec + e, comb, 0.0), axis=1, keepdims=True)
        ye = ffn(xb, wg_ref[e], wu_ref[e], wd_ref[e], col)
        y = ye if y is None else y + ye
    acc_ref[...] += y

    @pl.when(c == pl.num_programs(1) - 1)
    def _():
        o_ref[...] = _layer_norm(acc_ref[...], g_ref[...], b_ref[...])


def _moe(x1, comb, layer, wg, wu, wd, sg, su, sd, p, pw, pgw, pgb, g, b, alpha, tm, ec):
    t, d = x1.shape
    _, e, _, f = wg.shape
    dp = p.shape[2]
    expert = lambda i, c: (layer, c, 0, 0)
    return pl.pallas_call(
        functools.partial(_moe_kernel, alpha=alpha),
        grid=(t // tm, e // ec),
        in_specs=[pl.BlockSpec((tm, d), lambda i, c: (i, 0), pipeline_mode=pl.Buffered(1)),
                  pl.BlockSpec((tm, e), lambda i, c: (i, 0), pipeline_mode=pl.Buffered(1)),
                  pl.BlockSpec((None, ec, d, f), expert),
                  pl.BlockSpec((None, ec, d, f), expert),
                  pl.BlockSpec((None, ec, f, d), expert),
                  _layer_spec(sg, layer), _layer_spec(su, layer), _layer_spec(sd, layer),
                  pl.BlockSpec((None, tm, dp), lambda i, c: (layer, i, 0), pipeline_mode=pl.Buffered(1)),
                  _layer_spec(pw, layer), _layer_spec(pgw, layer), _layer_spec(pgb, layer),
                  _layer_spec(g, layer), _layer_spec(b, layer)],
        out_specs=pl.BlockSpec((tm, d), lambda i, c: (i, 0)),
        out_shape=jax.ShapeDtypeStruct((t, d), F32),
        scratch_shapes=[pltpu.VMEM((tm, d), BF16), pltpu.VMEM((tm, d), F32)],
        compiler_params=_params("parallel", "arbitrary"),
        name="moe",
    )(x1, comb, wg, wu, wd, sg, su, sd, p, pw, pgw, pgb, g, b)


def _pick_tile(n, target):
    tm = min(n, target)
    assert n % tm == 0, (n, tm)
    return tm


def kernel(x, p, w_in, conv_w, conv_b, lru_wa, lru_ba, lru_wx, lru_bx, lru_lambda, w_lru_out, s5_a_re, s5_a_im, s5_b_re, s5_b_im, s5_c_re, s5_c_im, s5_d, s5_log_dt, w_glu, b_glu, w_out, ln1_g, ln1_b, router_w, router_bias, moe_w_gate, moe_w_up, moe_w_down, shared_w_gate, shared_w_up, shared_w_down, ple_w, ple_gate_w, ple_gate_b, ln2_g, ln2_b):
    depth = w_in.shape[0]
    bsz, seq, d = x.shape
    t = bsz * seq
    d_rnn = conv_w.shape[-1]
    d_s5 = s5_d.shape[-1]
    alpha = (2.0 * depth) ** 0.25
    widths = (d_rnn, d_rnn, d_s5, d, d)
    assert sum(widths) == w_in.shape[-1]
    assert d_s5 % LANES == 0 and seq % S5_SUB == 0

    heads, dh = lru_wa.shape[1], lru_wa.shape[2]
    hpb = (LANES // math.gcd(dh, LANES))
    assert heads % hpb == 0
    ec = 4
    tm_in = _pick_tile(t, 512)
    tm_lru = _pick_tile(seq, 256)
    tm_mix = _pick_tile(t, 512)
    tm_moe = _pick_tile(t, 1024)
    n_pow = max(1, int(math.ceil(math.log2(seq // S5_SUB))))

    bf = lambda a: a.astype(BF16)
    w_in_b, w_lru_out_b, w_glu_b, w_out_b = bf(w_in), bf(w_lru_out), bf(w_glu), bf(w_out)
    wg_b, wu_b, wd_b = bf(moe_w_gate), bf(moe_w_up), bf(moe_w_down)
    sg_b, su_b, sd_b = bf(shared_w_gate), bf(shared_w_up), bf(shared_w_down)
    ple_w_b, ple_gate_w_b = bf(ple_w), bf(ple_gate_w)
    lru_params = (conv_w, _row3(conv_b), _block_diag_heads(lru_wa, hpb), _row3(lru_ba.reshape(depth, -1)),
                  _block_diag_heads(lru_wx, hpb), _row3(lru_bx.reshape(depth, -1)), _row3(lru_lambda), w_lru_out_b)
    mix_params = (w_glu_b, _row3(b_glu), w_out_b, _row3(ln1_g), _row3(ln1_b),
                  jnp.swapaxes(router_w, 1, 2), router_bias[..., None])

    groups, n_state = s5_a_re.shape[1], s5_a_re.shape[2]
    nc = d_s5 // LANES
    fold = lambda a: a.reshape((depth * groups,) + a.shape[2:])
    bd, s_re2, s_im2, cwc, (lre, lim) = _s5_operators(fold(s5_a_re), fold(s5_a_im), fold(s5_b_re), fold(s5_b_im),
                                                     fold(s5_c_re), fold(s5_c_im), fold(s5_log_dt))
    unfold = lambda a: a.reshape((depth, nc) + a.shape[1:])
    d_t = jnp.tile(s5_d.reshape(depth, nc, 1, LANES), (1, 1, 1, S5_SUB))
    s5_ops = (unfold(bd), unfold(s_re2), unfold(s_im2), unfold(cwc), unfold(_s5_wpow(lre, lim, n_pow)), d_t)
    assert groups == nc * S5_LANE_GROUPS and n_state * (LANES // n_state) == LANES

    h = x.reshape(t, d)
    p2 = p.reshape(depth, t, p.shape[-1])
    for i in range(depth):
        x_lru, g_lru, u_s5, gate_a, gate_b = _inproj(h, w_in_b, i, widths, (0, 0, S5_SUB, 0, 0), tm_in)
        m_a = _lru_branch(x_lru, g_lru, gate_a, i, *lru_params, bsz, tm_lru)
        ys = _s5_branch(u_s5, i, s5_ops, bsz)
        x1, comb = _mixout(ys, m_a, gate_b, h, i, mix_params, alpha, tm_mix)
        h = _moe(x1, comb, i, wg_b, wu_b, wd_b, sg_b, su_b, sd_b, p2, ple_w_b, ple_gate_w_b, _row3(ple_gate_b),
                 _row3(ln2_g), _row3(ln2_b), alpha, tm_moe, ec)
    return h.reshape(bsz, seq, d)
```

```python
import functools
import math

import jax
import jax.numpy as jnp
from jax import lax
from jax.experimental import pallas as pl
from jax.experimental.pallas import tpu as pltpu

F32 = jnp.float32
BF16 = jnp.bfloat16

LRU_C = 8.0
TOP_K = 8
N_EXPERT_GROUPS = 8
TOPK_GROUPS = 4
ROUTED_SCALE = 2.5
LN_EPS = 1e-5

LANES = 128
SUBLANES = 8
MXU_DIM = 256
VMEM_LIMIT_BYTES = 60 * 1024 * 1024

S5_SUB = 16
S5_LANE_GROUPS = 8


def _params(*sem):
    return pltpu.CompilerParams(dimension_semantics=sem, vmem_limit_bytes=VMEM_LIMIT_BYTES)


def _layer_spec(stacked, layer, single_buffer=False):
    nd = stacked.ndim - 1
    mode = {"pipeline_mode": pl.Buffered(1)} if single_buffer else {}
    return pl.BlockSpec((None,) + stacked.shape[1:], lambda *_: (layer,) + (0,) * nd, **mode)


def _row3(v):
    return v.reshape(v.shape[0], 1, -1)


def _gelu(x):
    return 0.5 * x * (1.0 + jnp.tanh(math.sqrt(2.0 / math.pi) * (x + 0.044715 * (x * x * x))))


def _sigmoid(x):
    return 1.0 / (1.0 + jnp.exp(-x))


def _layer_norm(v, g, b):
    mu = jnp.mean(v, axis=-1, keepdims=True)
    c = v - mu
    var = jnp.mean(c * c, axis=-1, keepdims=True)
    return c * lax.rsqrt(var + LN_EPS) * g + b


def _inproj_kernel(x_ref, w_ref, *refs, bounds):
    o_refs, z_scr = refs[:-1], refs[-1]
    xb = x_ref[...].astype(BF16)
    for o_ref, (lo, hi) in zip(o_refs, bounds):
        z = jnp.dot(xb, w_ref[:, lo:hi], preferred_element_type=F32)
        if len(o_ref.shape) == 3:
            ts = o_ref.shape[2] // LANES
            for c in range(o_ref.shape[0]):
                z_scr[c] = z[:, c * LANES:(c + 1) * LANES]
                for s in range(ts):
                    o_ref[c, :, s * LANES:(s + 1) * LANES] = z_scr[
                        c, pl.ds(s, o_ref.shape[1], stride=ts), :].astype(o_ref.dtype)
        else:
            o_ref[...] = z.astype(o_ref.dtype)


def _inproj(x, w_all, layer, widths, flat_ts, tm):
    t, d = x.shape
    bounds, lo = [], 0
    for w in widths:
        bounds.append((lo, lo + w))
        lo += w
    out_specs, out_shape, scr_w = [], [], LANES
    for w, ts in zip(widths, flat_ts):
        if ts:
            out_specs.append(pl.BlockSpec((w // LANES, tm // ts, ts * LANES), lambda i: (0, i, 0)))
            out_shape.append(jax.ShapeDtypeStruct((w // LANES, t // ts, ts * LANES), BF16))
            scr_w = max(scr_w, w)
        else:
            out_specs.append(pl.BlockSpec((tm, w), lambda i: (i, 0)))
            out_shape.append(jax.ShapeDtypeStruct((t, w), BF16))
    return pl.pallas_call(
        functools.partial(_inproj_kernel, bounds=tuple(bounds)),
        grid=(t // tm,),
        in_specs=[pl.BlockSpec((tm, d), lambda i: (i, 0)), _layer_spec(w_all, layer, single_buffer=True)],
        out_specs=out_specs,
        out_shape=out_shape,
        scratch_shapes=[pltpu.VMEM((scr_w // LANES, tm, LANES), F32)],
        compiler_params=_params("parallel"),
        name="inproj",
    )(x, w_all)


def _shift_rows(v, d, fill):
    n = v.shape[0]
    if d % SUBLANES == 0:
        head = jnp.broadcast_to(jnp.asarray(fill, v.dtype), (d,) + v.shape[1:])
        return jnp.concatenate([head, v[:n - d]], axis=0)
    row = lax.broadcasted_iota(jnp.int32, v.shape, 0)
    return jnp.where(row >= d, pltpu.roll(v, d, 0), fill)


def _scan_rows(a, b, h0, a_scr, b_scr, h_scr):
    n, c = a.shape
    ng = n // SUBLANES
    for ct in range(c // LANES):
        a_scr[ct] = a[:, ct * LANES:(ct + 1) * LANES]
        b_scr[ct] = b[:, ct * LANES:(ct + 1) * LANES]
    for ct in range(c // LANES):
        every = lambda s: pl.ds(s, ng, stride=SUBLANES)
        a_s = [a_scr[ct, every(s), :] for s in range(SUBLANES)]
        acum, hloc = [a_s[0]], [b_scr[ct, every(0), :]]
        for s in range(1, SUBLANES):
            hloc.append(a_s[s] * hloc[-1] + b_scr[ct, every(s), :])
            acum.append(a_s[s] * acum[-1])
        ga, gh = acum[-1], hloc[-1]
        d = 1
        while d < ng:
            gh = gh + ga * _shift_rows(gh, d, 0.0)
            ga = ga * _shift_rows(ga, d, 1.0)
            d *= 2
        h0c = h0[:, ct * LANES:(ct + 1) * LANES]
        after = gh + ga * h0c
        cin = _shift_rows(after, 1, h0c)
        for s in range(SUBLANES):
            h_scr[ct, every(s), :] = hloc[s] + acum[s] * cin
    return jnp.concatenate([h_scr[ct] for ct in range(c // LANES)], axis=1)


def _lru_kernel(x_ref, g_ref, ga_ref, cw_ref, cb_ref, wa_ref, ba_ref, wx_ref, bx_ref, lam_ref, wo_ref,
                o_ref, prev_ref, h_ref, a_scr, b_scr, hs_scr, *, n_blk, blk):
    i = pl.program_id(1)

    @pl.when(i == 0)
    def _():
        prev_ref[...] = jnp.zeros_like(prev_ref)
        h_ref[...] = jnp.zeros_like(h_ref)

    tm = x_ref.shape[0]
    x = x_ref[...].astype(F32)
    ext = jnp.concatenate([prev_ref[...], x], axis=0)
    prev_ref[...] = x[tm - SUBLANES:, :]
    kw = cw_ref.shape[0]
    xc = cb_ref[...] + cw_ref[kw - 1:kw, :] * x
    for j in range(1, kw):
        xc = xc + cw_ref[kw - 1 - j:kw - j, :] * pltpu.roll(ext, j, 0)[SUBLANES:, :]

    xcb = xc.astype(BF16)
    r_parts, i_parts = [], []
    for k in range(n_blk):
        xk = xcb[:, k * blk:(k + 1) * blk]
        r_parts.append(jnp.dot(xk, wa_ref[k], preferred_element_type=F32))
        i_parts.append(jnp.dot(xk, wx_ref[k], preferred_element_type=F32))
    r = _sigmoid(jnp.concatenate(r_parts, axis=1) + ba_ref[...])
    ig = _sigmoid(jnp.concatenate(i_parts, axis=1) + bx_ref[...])

    nl = -lam_ref[...]
    softplus = jnp.maximum(nl, 0.0) + jnp.log1p(jnp.exp(-jnp.abs(nl)))
    log_a = (-LRU_C) * r * softplus
    a = jnp.exp(log_a)
    v = 1.0 - a * a
    mult = jnp.where(v > 0.0, v * lax.rsqrt(v), 0.0)
    row = lax.broadcasted_iota(jnp.int32, a.shape, 0)
    mult = jnp.where(jnp.logical_and(row == 0, i == 0), 1.0, mult)
    b = mult * (ig * xc)

    h = _scan_rows(a, b, h_ref[0:1, :], a_scr, b_scr, hs_scr)
    h_ref[...] = jnp.broadcast_to(h[tm - 1:tm, :], h_ref.shape)

    y = (_gelu(g_ref[...].astype(F32)) * h).astype(BF16)
    ya = jnp.dot(y, wo_ref[...], preferred_element_type=F32)
    o_ref[...] = (_sigmoid(ga_ref[...].astype(F32)) * ya).astype(o_ref.dtype)


def _lru_branch(x_lru, g_lru, gate_a, layer, conv_w, conv_b, wa_blk, ba, wx_blk, bx, lam, wo_all, bsz, tm):
    t, c = x_lru.shape
    d = gate_a.shape[1]
    seq = t // bsz
    nt = seq // tm
    _, n_blk, blk, _ = wa_blk.shape
    row = lambda b, i: (b * nt + i, 0)
    params = (conv_w, conv_b, wa_blk, ba, wx_blk, bx, lam, wo_all)
    return pl.pallas_call(
        functools.partial(_lru_kernel, n_blk=n_blk, blk=blk),
        grid=(bsz, nt),
        in_specs=[pl.BlockSpec((tm, c), row), pl.BlockSpec((tm, c), row), pl.BlockSpec((tm, d), row)]
                 + [_layer_spec(a, layer) for a in params],
        out_specs=pl.BlockSpec((tm, d), row),
        out_shape=jax.ShapeDtypeStruct((t, d), BF16),
        scratch_shapes=[pltpu.VMEM((SUBLANES, c), F32), pltpu.VMEM((SUBLANES, c), F32)]
                       + [pltpu.VMEM((c // LANES, tm, LANES), F32)] * 3,
        compiler_params=_params("arbitrary", "arbitrary"),
        name="lru_branch",
    )(x_lru, g_lru, gate_a, *params)


def _block_diag_heads(w, heads_per_blk):
    l, h, dh, _ = w.shape
    nb = h // heads_per_blk
    eye = jnp.eye(heads_per_blk, dtype=w.dtype)
    wb = w.reshape(l * nb, heads_per_blk, dh, dh)
    out = jnp.einsum("nhij,hk->nhikj", wb, eye)
    return out.reshape(l, nb, heads_per_blk * dh, heads_per_blk * dh).astype(BF16)


def _s5_lag_kernel(cr_ref, ci_ref, br_ref, bi_ref, pr_ref, pi_ref, k_ref):
    gb, p, n = cr_ref.shape
    ts = k_ref.shape[1]
    for g in range(gb):
        cr, ci = cr_ref[g][None], ci_ref[g][None]
        pr, pim = pr_ref[g, :ts], pi_ref[g, :ts]
        are = (cr * pr - ci * pim).reshape(ts * p, n)
        aim = (cr * pim + ci * pr).reshape(ts * p, n)
        k = (jnp.dot(are, br_ref[g], preferred_element_type=F32, precision=lax.Precision.HIGHEST)
             - jnp.dot(aim, bi_ref[g], preferred_element_type=F32, precision=lax.Precision.HIGHEST))
        k_ref[g] = k.reshape(ts, p, p)


def _s5_lag_kernels(cr, ci, bbr, bbi, pwr, pwi):
    g, p, n = cr.shape
    ts = pwr.shape[1] - 1
    gb = S5_LANE_GROUPS
    cspec = pl.BlockSpec((gb, p, n), lambda i: (i, 0, 0))
    bspec = pl.BlockSpec((gb, n, p), lambda i: (i, 0, 0))
    pspec = pl.BlockSpec((gb, ts + 1, 1, n), lambda i: (i, 0, 0, 0))
    return pl.pallas_call(
        _s5_lag_kernel,
        grid=(g // gb,),
        in_specs=[cspec, cspec, bspec, bspec, pspec, pspec],
        out_specs=pl.BlockSpec((gb, ts, p, p), lambda i: (i, 0, 0, 0)),
        out_shape=jax.ShapeDtypeStruct((g, ts, p, p), F32),
        compiler_params=_params("parallel"),
        name="s5_lag_kernels",
    )(cr, ci, bbr, bbi, pwr[:, :, None, :], pwi[:, :, None, :])


def _s5_operators(a_re, a_im, b_re, b_im, c_re, c_im, log_dt):
    ts, gl = S5_SUB, S5_LANE_GROUPS
    g, n = a_re.shape
    p = b_re.shape[-1]
    nc = g // gl
    dt = jnp.exp(log_dt)[:, None]
    lre, lim = dt * a_re, dt * a_im
    mag = jnp.exp(lre)
    abar_re, abar_im = mag * jnp.cos(lim), mag * jnp.sin(lim)
    den = a_re * a_re + a_im * a_im
    z_re = ((abar_re - 1.0) * a_re + abar_im * a_im) / den
    z_im = (abar_im * a_re - (abar_re - 1.0) * a_im) / den
    bb_re = z_re[..., None] * b_re - z_im[..., None] * b_im
    bb_im = z_re[..., None] * b_im + z_im[..., None] * b_re
    steps = jnp.arange(ts + 1, dtype=F32)[None, :, None]
    pmag = jnp.exp(steps * lre[:, None, :])
    pw_re = pmag * jnp.cos(steps * lim[:, None, :])
    pw_im = pmag * jnp.sin(steps * lim[:, None, :])

    k = _s5_lag_kernels(c_re, c_im, bb_re, bb_im, pw_re, pw_im)
    bd = jnp.swapaxes(k, 2, 3).reshape(nc, gl, ts, p, p).transpose(0, 2, 1, 3, 4).reshape(nc, ts, gl * p, p)

    rev_re, rev_im = pw_re[:, ts - 1::-1][:, :ts], pw_im[:, ts - 1::-1][:, :ts]
    bt_re, bt_im = jnp.swapaxes(bb_re, 1, 2), jnp.swapaxes(bb_im, 1, 2)
    sb_re = rev_re[:, :, None, :] * bt_re[:, None] - rev_im[:, :, None, :] * bt_im[:, None]
    sb_im = rev_re[:, :, None, :] * bt_im[:, None] + rev_im[:, :, None, :] * bt_re[:, None]

    def rows_sgq(a):
        a = a.reshape(nc, gl, ts, p, n).transpose(0, 2, 1, 3, 4).reshape(nc, ts * gl * p, n)
        return jnp.concatenate([a] * (LANES // n), axis=-1)
    s_re2, s_im2 = rows_sgq(sb_re), rows_sgq(sb_im)

    c4 = lambda a: a.reshape(nc, gl, p, n).transpose(0, 3, 1, 2)[:, :, None]
    p4 = lambda a: a[:, 1:].reshape(nc, gl, ts, n).transpose(0, 3, 2, 1)[..., None]
    ct_re = c4(c_re) * p4(pw_re) - c4(c_im) * p4(pw_im)
    ct_im = c4(c_re) * p4(pw_im) + c4(c_im) * p4(pw_re)
    cwc = jnp.stack([ct_re, -ct_im], axis=1).reshape(nc, 2, n, ts * gl * p)
    return bd, s_re2, s_im2, cwc, (lre, lim)


def _s5_wpow(lre, lim, n_pow):
    ts, gl = S5_SUB, S5_LANE_GROUPS
    g, n = lre.shape
    nc = g // gl
    e = (ts * (2.0 ** jnp.arange(n_pow, dtype=F32)))[None, :, None]
    mag = jnp.exp(e * lre[:, None, :])
    wr = (mag * jnp.cos(e * lim[:, None, :])).reshape(nc, gl, n_pow, n)
    wi = (mag * jnp.sin(e * lim[:, None, :])).reshape(nc, gl, n_pow, n)
    w = jnp.stack([wr, wi], axis=0)
    return jnp.transpose(w, (1, 3, 0, 2, 4)).reshape(nc, n_pow, 2 * gl * n)


def _s5_assemble(bd_ref, sre_ref, sim_ref, cw_ref, m_scr, s_scr, c_scr):
    ts = bd_ref.shape[1]
    n = cw_ref.shape[2]
    half = s_scr.shape[1] // 2
    gl = half // n
    p = LANES // gl
    square = (LANES, LANES)
    spread = (lax.broadcasted_iota(jnp.int32, (p, LANES), 1) % p == lax.broadcasted_iota(jnp.int32, (p, LANES), 0))
    spread = jnp.where(spread, 1.0, 0.0).astype(m_scr.dtype)
    same_group = lax.broadcasted_iota(jnp.int32, square, 0) // p == lax.broadcasted_iota(jnp.int32, square, 1) // p
    lag_blocks = [jnp.where(same_group, jnp.dot(bd_ref[0, j].astype(m_scr.dtype), spread, preferred_element_type=F32),
                            0.0).astype(m_scr.dtype) for j in range(ts)]
    for s in range(ts):
        for t in range(ts):
            blk = (slice(s * LANES, (s + 1) * LANES), slice(t * LANES, (t + 1) * LANES))
            if t >= s:
                m_scr[blk] = lag_blocks[t - s]
            elif t * LANES // MXU_DIM == s * LANES // MXU_DIM:
                m_scr[blk] = jnp.zeros((LANES, LANES), m_scr.dtype)
    shape = (s_scr.shape[0], half)
    own = ((lax.broadcasted_iota(jnp.int32, shape, 0) // p) % gl) == (lax.broadcasted_iota(jnp.int32, shape, 1) // n)
    for ref, lo in ((sre_ref, 0), (sim_ref, half)):
        wide = jnp.concatenate([ref[0]] * (half // LANES), axis=1)
        s_scr[:, lo:lo + half] = jnp.where(own, wide, 0.0).astype(s_scr.dtype)
    colg = (lax.broadcasted_iota(jnp.int32, (n, c_scr.shape[1]), 1) // p) % gl
    for r in range(2):
        for g in range(gl):
            c_scr[r * half + g * n:r * half + (g + 1) * n, :] = jnp.where(colg == g, cw_ref[0, r], 0.0).astype(c_scr.dtype)


def _s5_kernel(x_ref, bd_ref, sre_ref, sim_ref, cw_ref, w_ref, d_ref, y_ref, m_scr, s_scr, c_scr, *, n_pow):
    @pl.when(pl.program_id(1) == 0)
    def _():
        _s5_assemble(bd_ref, sre_ref, sim_ref, cw_ref, m_scr, s_scr, c_scr)

    x = x_ref[0]
    ds = jnp.dot(x, s_scr[...], preferred_element_type=F32)
    half = ds.shape[1] // 2
    rows = ds.shape[0]
    hr, hi = ds[:, :half], ds[:, half:]
    row = lax.broadcasted_iota(jnp.int32, hr.shape, 0)
    d = 1
    for k in range(n_pow):
        if d >= rows:
            break
        wr, wi = w_ref[0, k:k + 1, :half], w_ref[0, k:k + 1, half:]
        if d % SUBLANES == 0:
            sr, si = hr[:rows - d], hi[:rows - d]
            hr = jnp.concatenate([hr[:d], hr[d:] + (wr * sr - wi * si)], axis=0)
            hi = jnp.concatenate([hi[:d], hi[d:] + (wr * si + wi * sr)], axis=0)
        else:
            m = row >= d
            sr = jnp.where(m, pltpu.roll(hr, d, 0), 0.0)
            si = jnp.where(m, pltpu.roll(hi, d, 0), 0.0)
            hr, hi = hr + (wr * sr - wi * si), hi + (wr * si + wi * sr)
        d *= 2
    m = row >= 1
    hr = jnp.where(m, pltpu.roll(hr, 1, 0), 0.0)
    hi = jnp.where(m, pltpu.roll(hi, 1, 0), 0.0)
    hs = jnp.concatenate([hr, hi], axis=1).astype(BF16)
    for j in range(x.shape[1] // MXU_DIM):
        lo, hi_ = j * MXU_DIM, (j + 1) * MXU_DIM
        y = (jnp.dot(x[:, :hi_], m_scr[:hi_, lo:hi_], preferred_element_type=F32)
             + jnp.dot(hs, c_scr[:, lo:hi_], preferred_element_type=F32))
        y = y + d_ref[0, :, lo:hi_] * x[:, lo:hi_].astype(F32)
        y_ref[0, :, lo:hi_] = _gelu(y).astype(y_ref.dtype)


def _s5_branch(u2, layer, ops, bsz):
    nc, n_sub, width = u2.shape
    rows = n_sub // bsz
    cwc, wpow = ops[3], ops[4]
    n_state2 = 2 * S5_LANE_GROUPS * cwc.shape[3]
    x_spec = pl.BlockSpec((1, rows, width), lambda c, b: (c, b, 0))
    op_spec = lambda a: pl.BlockSpec((None, 1) + a.shape[2:], lambda c, b: (layer, c) + (0,) * (a.ndim - 2))
    return pl.pallas_call(
        functools.partial(_s5_kernel, n_pow=wpow.shape[2]),
        grid=(nc, bsz),
        in_specs=[x_spec] + [op_spec(a) for a in ops],
        out_specs=x_spec,
        out_shape=jax.ShapeDtypeStruct(u2.shape, BF16),
        scratch_shapes=[pltpu.VMEM((width, width), BF16), pltpu.VMEM((width, n_state2), BF16),
                        pltpu.VMEM((n_state2, width), BF16)],
        compiler_params=_params("arbitrary", "arbitrary"),
        name="s5_branch",
    )(u2, *ops)


def _route(logits_t, bias):
    e, tm = logits_t.shape
    per = e // N_EXPERT_GROUPS
    scores = _sigmoid(logits_t)
    biased = scores + bias
    g3 = biased.reshape(N_EXPERT_GROUPS, per, tm)
    sub = lax.broadcasted_iota(jnp.int32, g3.shape, 1)
    m1 = jnp.max(g3, axis=1, keepdims=True)
    first = jnp.min(jnp.where(g3 == m1, sub, per), axis=1, keepdims=True)
    m2 = jnp.max(jnp.where(sub == first, -jnp.inf, g3), axis=1, keepdims=True)
    gs = (m1 + m2).reshape(N_EXPERT_GROUPS, tm)
    gidx = lax.broadcasted_iota(jnp.int32, gs.shape, 0)
    grank = jnp.zeros(gs.shape, jnp.int32)
    for j in range(N_EXPERT_GROUPS):
        other = gs[j:j + 1, :]
        ahead = jnp.logical_or(other > gs, jnp.logical_and(other == gs, j < gidx))
        grank = grank + ahead.astype(jnp.int32)
    gsel = (grank < TOPK_GROUPS).reshape(N_EXPERT_GROUPS, 1, tm)
    masked = jnp.where(gsel, g3, -jnp.inf).reshape(e, tm)
    eidx = lax.broadcasted_iota(jnp.int32, masked.shape, 0)
    erank = jnp.zeros(masked.shape, jnp.int32)
    for j in range(e):
        other = masked[j:j + 1, :]
        ahead = jnp.logical_or(other > masked, jnp.logical_and(other == masked, j < eidx))
        erank = erank + ahead.astype(jnp.int32)
    sel = erank < TOP_K
    gate = jnp.where(sel, scores, 0.0)
    denom = jnp.sum(gate, axis=0, keepdims=True)
    return ROUTED_SCALE * gate / denom


def _mixout_kernel(ys_ref, ma_ref, gb_ref, x_ref, wglu_ref, bglu_ref, wout_ref, g_ref, b_ref, rw_ref, rb_ref,
                   x1_ref, comb_ref, ys_scr, *, alpha):
    d = x_ref.shape[1]
    nc, n_sub, width = ys_ref.shape
    ts = width // LANES
    for c in range(nc):
        for s in range(ts):
            ys_scr[c, pl.ds(s, n_sub, stride=ts), :] = ys_ref[c, :, s * LANES:(s + 1) * LANES].astype(F32)
    ys = jnp.concatenate([ys_scr[c] for c in range(nc)], axis=1).astype(BF16)
    glu = jnp.dot(ys, wglu_ref[...], preferred_element_type=F32) + bglu_ref[...]
    yb = glu[:, :d] * _sigmoid(glu[:, d:])
    merged = ma_ref[...].astype(F32) + _sigmoid(gb_ref[...].astype(F32)) * yb
    mix = jnp.dot(merged.astype(BF16), wout_ref[...], preferred_element_type=F32)
    x1 = _layer_norm(alpha * x_ref[...] + mix, g_ref[...], b_ref[...])
    x1_ref[...] = x1
    rw = rw_ref[...]
    rw_hi = rw.astype(BF16)
    rw_lo = (rw - rw_hi.astype(F32)).astype(BF16)
    x_hi = x1.astype(BF16)
    x_lo = (x1 - x_hi.astype(F32)).astype(BF16)
    nt = (((1,), (1,)), ((), ()))
    logits_t = (lax.dot_general(rw_hi, x_hi, nt, preferred_element_type=F32)
                + lax.dot_general(rw_hi, x_lo, nt, preferred_element_type=F32)
                + lax.dot_general(rw_lo, x_hi, nt, preferred_element_type=F32))
    comb_ref[...] = _route(logits_t, rb_ref[...]).T


def _mixout(ys, ma, gb, x, layer, params, alpha, tm):
    t, d = x.shape
    e = params[5].shape[1]
    nc, _, width = ys.shape
    ts = width // LANES
    row = lambda i: (i, 0)
    return pl.pallas_call(
        functools.partial(_mixout_kernel, alpha=alpha),
        grid=(t // tm,),
        in_specs=[pl.BlockSpec((nc, tm // ts, width), lambda i: (0, i, 0))] +
                 [pl.BlockSpec((tm, d), row)] * 3 + [_layer_spec(a, layer) for a in params],
        out_specs=[pl.BlockSpec((tm, d), row), pl.BlockSpec((tm, e), row)],
        out_shape=[jax.ShapeDtypeStruct((t, d), F32), jax.ShapeDtypeStruct((t, e), F32)],
        scratch_shapes=[pltpu.VMEM((nc, tm, LANES), F32)],
        compiler_params=_params("parallel"),
        name="mixout",
    )(ys, ma, gb, x, *params)


def _moe_kernel(x_ref, comb_ref, wg_ref, wu_ref, wd_ref, sg_ref, su_ref, sd_ref, p_ref, pw_ref, pgw_ref, pgb_ref,
                g_ref, b_ref, o_ref, xb_ref, acc_ref, *, alpha):
    c = pl.program_id(1)
    ec = wg_ref.shape[0]

    def ffn(xb, wg, wu, wd, scale):
        gg = jnp.dot(xb, wg, preferred_element_type=F32)
        hdn = gg * _sigmoid(gg) * jnp.dot(xb, wu, preferred_element_type=F32)
        if scale is not None:
            hdn = hdn * scale
        return jnp.dot(hdn.astype(BF16), wd, preferred_element_type=F32)

    @pl.when(c == 0)
    def _():
        x = x_ref[...]
        xb = x.astype(BF16)
        xb_ref[...] = xb
        gate = _sigmoid(jnp.dot(xb, pgw_ref[...], preferred_element_type=F32) + pgb_ref[...])
        ple = gate * jnp.dot(p_ref[...].astype(BF16), pw_ref[...], preferred_element_type=F32)
        acc_ref[...] = alpha * x + ple + ffn(xb, sg_ref[...], su_ref[...], sd_ref[...], None)

    xb = xb_ref[...]
    comb = comb_ref[...]
    lane = lax.broadcasted_iota(jnp.int32, comb.shape, 1)
    y = None
    for e in range(ec):
        col = jnp.sum(jnp.where(lane == c * ec + e, comb, 0.0), axis=1, keepdims=True)
        ye = ffn(xb, wg_ref[e], wu_ref[e], wd_ref[e], col)
        y = ye if y is None else y + ye
    acc_ref[...] += y

    @pl.when(c == pl.num_programs(1) - 1)
    def _():
        o_ref[...] = _layer_norm(acc_ref[...], g_ref[...], b_ref[...])


def _moe(x1, comb, layer, wg, wu, wd, sg, su, sd, p, pw, pgw, pgb, g, b, alpha, tm, ec):
    t, d = x1.shape
    _, e, _, f = wg.shape
    dp = p.shape[2]
    expert = lambda i, c: (layer, c, 0, 0)
    return pl.pallas_call(
        functools.partial(_moe_kernel, alpha=alpha),
        grid=(t // tm, e // ec),
        in_specs=[pl.BlockSpec((tm, d), lambda i, c: (i, 0)),
                  pl.BlockSpec((tm, e), lambda i, c: (i, 0)),
                  pl.BlockSpec((None, ec, d, f), expert),
                  pl.BlockSpec((None, ec, d, f), expert),
                  pl.BlockSpec((None, ec, f, d), expert),
                  _layer_spec(sg, layer), _layer_spec(su, layer), _layer_spec(sd, layer),
                  pl.BlockSpec((None, tm, dp), lambda i, c: (layer, i, 0)),
                  _layer_spec(pw, layer), _layer_spec(pgw, layer), _layer_spec(pgb, layer),
                  _layer_spec(g, layer), _layer_spec(b, layer)],
        out_specs=pl.BlockSpec((tm, d), lambda i, c: (i, 0)),
        out_shape=jax.ShapeDtypeStruct((t, d), F32),
        scratch_shapes=[pltpu.VMEM((tm, d), BF16), pltpu.VMEM((tm, d), F32)],
        compiler_params=_params("parallel", "arbitrary"),
        name="moe",
    )(x1, comb, wg, wu, wd, sg, su, sd, p, pw, pgw, pgb, g, b)


def _pick_tile(n, target):
    tm = min(n, target)
    assert n % tm == 0, (n, tm)
    return tm


def kernel(x, p, w_in, conv_w, conv_b, lru_wa, lru_ba, lru_wx, lru_bx, lru_lambda, w_lru_out, s5_a_re, s5_a_im, s5_b_re, s5_b_im, s5_c_re, s5_c_im, s5_d, s5_log_dt, w_glu, b_glu, w_out, ln1_g, ln1_b, router_w, router_bias, moe_w_gate, moe_w_up, moe_w_down, shared_w_gate, shared_w_up, shared_w_down, ple_w, ple_gate_w, ple_gate_b, ln2_g, ln2_b):
    depth = w_in.shape[0]
    bsz, seq, d = x.shape
    t = bsz * seq
    d_rnn = conv_w.shape[-1]
    d_s5 = s5_d.shape[-1]
    alpha = (2.0 * depth) ** 0.25
    widths = (d_rnn, d_rnn, d_s5, d, d)
    assert sum(widths) == w_in.shape[-1]
    assert d_s5 % LANES == 0 and seq % S5_SUB == 0

    heads, dh = lru_wa.shape[1], lru_wa.shape[2]
    hpb = (LANES // math.gcd(dh, LANES))
    assert heads % hpb == 0
    ec = 4
    tm_in = _pick_tile(t, 512)
    tm_lru = _pick_tile(seq, 256)
    tm_mix = _pick_tile(t, 512)
    tm_moe = _pick_tile(t, 1024)
    n_pow = max(1, int(math.ceil(math.log2(seq // S5_SUB))))

    bf = lambda a: a.astype(BF16)
    w_in_b, w_lru_out_b, w_glu_b, w_out_b = bf(w_in), bf(w_lru_out), bf(w_glu), bf(w_out)
    wg_b, wu_b, wd_b = bf(moe_w_gate), bf(moe_w_up), bf(moe_w_down)
    sg_b, su_b, sd_b = bf(shared_w_gate), bf(shared_w_up), bf(shared_w_down)
    ple_w_b, ple_gate_w_b = bf(ple_w), bf(ple_gate_w)
    lru_params = (conv_w, _row3(conv_b), _block_diag_heads(lru_wa, hpb), _row3(lru_ba.reshape(depth, -1)),
                  _block_diag_heads(lru_wx, hpb), _row3(lru_bx.reshape(depth, -1)), _row3(lru_lambda), w_lru_out_b)
    mix_params = (w_glu_b, _row3(b_glu), w_out_b, _row3(ln1_g), _row3(ln1_b),
                  jnp.swapaxes(router_w, 1, 2), router_bias[..., None])

    groups, n_state = s5_a_re.shape[1], s5_a_re.shape[2]
    nc = d_s5 // LANES
    fold = lambda a: a.reshape((depth * groups,) + a.shape[2:])
    bd, s_re2, s_im2, cwc, (lre, lim) = _s5_operators(fold(s5_a_re), fold(s5_a_im), fold(s5_b_re), fold(s5_b_im),
                                                     fold(s5_c_re), fold(s5_c_im), fold(s5_log_dt))
    unfold = lambda a: a.reshape((depth, nc) + a.shape[1:])
    d_t = jnp.tile(s5_d.reshape(depth, nc, 1, LANES), (1, 1, 1, S5_SUB))
    s5_ops = (unfold(bd), unfold(s_re2), unfold(s_im2), unfold(cwc), unfold(_s5_wpow(lre, lim, n_pow)), d_t)
    assert groups == nc * S5_LANE_GROUPS and n_state * (LANES // n_state) == LANES

    h = x.reshape(t, d)
    p2 = p.reshape(depth, t, p.shape[-1])
    for i in range(depth):
        x_lru, g_lru, u_s5, gate_a, gate_b = _inproj(h, w_in_b, i, widths, (0, 0, S5_SUB, 0, 0), tm_in)
        m_a = _lru_branch(x_lru, g_lru, gate_a, i, *lru_params, bsz, tm_lru)
        ys = _s5_branch(u_s5, i, s5_ops, bsz)
        x1, comb = _mixout(ys, m_a, gate_b, h, i, mix_params, alpha, tm_mix)
        h = _moe(x1, comb, i, wg_b, wu_b, wd_b, sg_b, su_b, sd_b, p2, ple_w_b, ple_gate_w_b, _row3(ple_gate_b),
                 _row3(ln2_g), _row3(ln2_b), alpha, tm_moe, ec)
    return h.reshape(bsz, seq, d)
```

```python
import functools
import math

import jax
import jax.numpy as jnp
from jax import lax
from jax.experimental import pallas as pl
from jax.experimental.pallas import tpu as pltpu

F32 = jnp.float32
BF16 = jnp.bfloat16

LRU_C = 8.0
TOP_K = 8
N_EXPERT_GROUPS = 8
TOPK_GROUPS = 4
ROUTED_SCALE = 2.5
LN_EPS = 1e-5

LANES = 128
SUBLANES = 8
MXU_DIM = 256
PROJ_CHUNK = 2 * MXU_DIM
VMEM_LIMIT_BYTES = 60 * 1024 * 1024

S5_SUB = 16
S5_LANE_GROUPS = 8


def _params(*sem):
    return pltpu.CompilerParams(dimension_semantics=sem, vmem_limit_bytes=VMEM_LIMIT_BYTES)


def _layer_spec(stacked, layer, single_buffer=False):
    nd = stacked.ndim - 1
    mode = {"pipeline_mode": pl.Buffered(1)} if single_buffer else {}
    return pl.BlockSpec((None,) + stacked.shape[1:], lambda *_: (layer,) + (0,) * nd, **mode)


def _row3(v):
    return v.reshape(v.shape[0], 1, -1)


def _gelu(x):
    return 0.5 * x * (1.0 + jnp.tanh(math.sqrt(2.0 / math.pi) * (x + 0.044715 * (x * x * x))))


def _sigmoid(x):
    return 1.0 / (1.0 + jnp.exp(-x))


def _layer_norm(v, g, b):
    mu = jnp.mean(v, axis=-1, keepdims=True)
    c = v - mu
    var = jnp.mean(c * c, axis=-1, keepdims=True)
    return c * lax.rsqrt(var + LN_EPS) * g + b


def _shift_rows(v, d, fill):
    n = v.shape[0]
    if d % SUBLANES == 0:
        head = jnp.broadcast_to(jnp.asarray(fill, v.dtype), (d,) + v.shape[1:])
        return jnp.concatenate([head, v[:n - d]], axis=0)
    row = lax.broadcasted_iota(jnp.int32, v.shape, 0)
    return jnp.where(row >= d, pltpu.roll(v, d, 0), fill)


def _scan_rows(a, b, h0, a_scr, b_scr, h_scr):
    n, c = a.shape
    ng = n // SUBLANES
    for ct in range(c // LANES):
        a_scr[ct] = a[:, ct * LANES:(ct + 1) * LANES]
        b_scr[ct] = b[:, ct * LANES:(ct + 1) * LANES]
    for ct in range(c // LANES):
        every = lambda s: pl.ds(s, ng, stride=SUBLANES)
        a_s = [a_scr[ct, every(s), :] for s in range(SUBLANES)]
        acum, hloc = [a_s[0]], [b_scr[ct, every(0), :]]
        for s in range(1, SUBLANES):
            hloc.append(a_s[s] * hloc[-1] + b_scr[ct, every(s), :])
            acum.append(a_s[s] * acum[-1])
        ga, gh = acum[-1], hloc[-1]
        d = 1
        while d < ng:
            gh = gh + ga * _shift_rows(gh, d, 0.0)
            ga = ga * _shift_rows(ga, d, 1.0)
            d *= 2
        h0c = h0[:, ct * LANES:(ct + 1) * LANES]
        after = gh + ga * h0c
        cin = _shift_rows(after, 1, h0c)
        for s in range(SUBLANES):
            h_scr[ct, every(s), :] = hloc[s] + acum[s] * cin
    return jnp.concatenate([h_scr[ct] for ct in range(c // LANES)], axis=1)


def _lru_tile(x_ref, g_ref, ga_ref, first, cw_ref, cb_ref, wa_ref, ba_ref, wx_ref, bx_ref, lam_ref, wo_ref,
              prev_ref, h_ref, a_scr, b_scr, hs_scr, n_blk, blk, side_work):
    sites = 5
    per_phase = -(-len(side_work) // sites)

    def run_side_work():
        for _ in range(min(per_phase, len(side_work))):
            side_work.pop(0)()

    tm = x_ref.shape[0]
    run_side_work()
    x = x_ref[...].astype(F32)
    ext = jnp.concatenate([prev_ref[...], x], axis=0)
    prev_ref[...] = x[tm - SUBLANES:, :]
    kw = cw_ref.shape[0]
    xc = cb_ref[...] + cw_ref[kw - 1:kw, :] * x
    for j in range(1, kw):
        xc = xc + cw_ref[kw - 1 - j:kw - j, :] * pltpu.roll(ext, j, 0)[SUBLANES:, :]

    run_side_work()
    xcb = xc.astype(BF16)
    r_parts, i_parts = [], []
    for k in range(n_blk):
        xk = xcb[:, k * blk:(k + 1) * blk]
        r_parts.append(jnp.dot(xk, wa_ref[k], preferred_element_type=F32))
        i_parts.append(jnp.dot(xk, wx_ref[k], preferred_element_type=F32))
    r = _sigmoid(jnp.concatenate(r_parts, axis=1) + ba_ref[...])
    ig = _sigmoid(jnp.concatenate(i_parts, axis=1) + bx_ref[...])

    run_side_work()
    nl = -lam_ref[...]
    softplus = jnp.maximum(nl, 0.0) + jnp.log1p(jnp.exp(-jnp.abs(nl)))
    log_a = (-LRU_C) * r * softplus
    a = jnp.exp(log_a)
    v = 1.0 - a * a
    mult = jnp.where(v > 0.0, v * lax.rsqrt(v), 0.0)
    row = lax.broadcasted_iota(jnp.int32, a.shape, 0)
    mult = jnp.where(jnp.logical_and(row == 0, first), 1.0, mult)
    b = mult * (ig * xc)

    run_side_work()
    h = _scan_rows(a, b, h_ref[0:1, :], a_scr, b_scr, hs_scr)
    h_ref[...] = jnp.broadcast_to(h[tm - 1:tm, :], h_ref.shape)

    run_side_work()
    y = (_gelu(g_ref[...].astype(F32)) * h).astype(BF16)
    ya = jnp.dot(y, wo_ref[...], preferred_element_type=F32)
    out = _sigmoid(ga_ref[...].astype(F32)) * ya
    while side_work:
        side_work.pop(0)()
    return out


def _inlru_kernel(x_ref, w_ref, cw_ref, cb_ref, wa_ref, ba_ref, wx_ref, bx_ref, lam_ref, wo_ref,
                  u_ref, gb_ref, ma_ref,
                  xl_scr, gl_scr, ga_scr, z_scr, prev_ref, h_ref, a_scr, b_scr, hs_scr, *, bounds, n_blk, blk):
    j = pl.program_id(1)

    @pl.when(j == 0)
    def _():
        xl_scr[...] = jnp.zeros_like(xl_scr)
        gl_scr[...] = jnp.zeros_like(gl_scr)
        ga_scr[...] = jnp.zeros_like(ga_scr)

    @pl.when(j <= 1)
    def _():
        prev_ref[...] = jnp.zeros_like(prev_ref)
        h_ref[...] = jnp.zeros_like(h_ref)

    fill, drain = j % 2, (j + 1) % 2
    xb = x_ref[...].astype(BF16)
    ts = u_ref.shape[2] // LANES

    def project(lo, hi, store):
        def thunk():
            store(jnp.dot(xb, w_ref[:, lo:hi], preferred_element_type=F32))
        return thunk

    def to_stash(scr, off):
        def store(z):
            scr[fill, :, off:off + z.shape[1]] = z.astype(scr.dtype)
        return store

    def to_gate_b(off):
        def store(z):
            gb_ref[:, off:off + z.shape[1]] = z.astype(gb_ref.dtype)
        return store

    def to_s5(off):
        def store(z):
            for k in range(z.shape[1] // LANES):
                c = off // LANES + k
                z_scr[c] = z[:, k * LANES:(k + 1) * LANES]
                for s in range(ts):
                    u_ref[c, :, s * LANES:(s + 1) * LANES] = z_scr[
                        c, pl.ds(s, u_ref.shape[1], stride=ts), :].astype(u_ref.dtype)
        return store

    sinks = (functools.partial(to_stash, xl_scr), functools.partial(to_stash, gl_scr), to_s5,
             functools.partial(to_stash, ga_scr), to_gate_b)
    side_work = []
    for (lo, hi), sink in zip(bounds, sinks):
        for c0 in range(lo, hi, PROJ_CHUNK):
            side_work.append(project(c0, min(c0 + PROJ_CHUNK, hi), sink(c0 - lo)))

    ma = _lru_tile(xl_scr.at[drain], gl_scr.at[drain], ga_scr.at[drain], j == 1,
                   cw_ref, cb_ref, wa_ref, ba_ref, wx_ref, bx_ref, lam_ref, wo_ref,
                   prev_ref, h_ref, a_scr, b_scr, hs_scr, n_blk, blk, side_work)
    ma_ref[...] = ma.astype(ma_ref.dtype)


def _inproj_lru(x, layer, w_all, lru_params, widths, ts, bsz, tm):
    t, d = x.shape
    d_rnn, _, d_s5, d_a, d_b = widths
    seq = t // bsz
    nt = seq // tm
    _, n_blk, blk, _ = lru_params[2].shape
    bounds, lo = [], 0
    for w in widths:
        bounds.append((lo, lo + w))
        lo += w
    cur = lambda b, j: (b * nt + jnp.minimum(j, nt - 1), 0)
    prv = lambda b, j: (b * nt + jnp.maximum(j - 1, 0), 0)
    nc = d_s5 // LANES
    return pl.pallas_call(
        functools.partial(_inlru_kernel, bounds=tuple(bounds), n_blk=n_blk, blk=blk),
        grid=(bsz, nt + 1),
        in_specs=[pl.BlockSpec((tm, d), cur), _layer_spec(w_all, layer, single_buffer=True)]
                 + [_layer_spec(a, layer) for a in lru_params],
        out_specs=[pl.BlockSpec((nc, tm // ts, ts * LANES), lambda b, j: (0,) + cur(b, j)),
                   pl.BlockSpec((tm, d_b), cur), pl.BlockSpec((tm, d_a), prv)],
        out_shape=[jax.ShapeDtypeStruct((nc, t // ts, ts * LANES), BF16),
                   jax.ShapeDtypeStruct((t, d_b), BF16), jax.ShapeDtypeStruct((t, d_a), BF16)],
        scratch_shapes=[pltpu.VMEM((2, tm, d_rnn), BF16), pltpu.VMEM((2, tm, d_rnn), BF16),
                        pltpu.VMEM((2, tm, d_a), BF16),
                        pltpu.VMEM((nc, tm, LANES), F32),
                        pltpu.VMEM((SUBLANES, d_rnn), F32), pltpu.VMEM((SUBLANES, d_rnn), F32)]
                       + [pltpu.VMEM((d_rnn // LANES, tm, LANES), F32)] * 3,
        compiler_params=_params("arbitrary", "arbitrary"),
        name="inproj_lru",
    )(x, w_all, *lru_params)


def _block_diag_heads(w, heads_per_blk):
    l, h, dh, _ = w.shape
    nb = h // heads_per_blk
    eye = jnp.eye(heads_per_blk, dtype=w.dtype)
    wb = w.reshape(l * nb, heads_per_blk, dh, dh)
    out = jnp.einsum("nhij,hk->nhikj", wb, eye)
    return out.reshape(l, nb, heads_per_blk * dh, heads_per_blk * dh).astype(BF16)


def _s5_lag_kernel(cr_ref, ci_ref, br_ref, bi_ref, pr_ref, pi_ref, k_ref):
    gb, p, n = cr_ref.shape
    ts = k_ref.shape[1]
    for g in range(gb):
        cr, ci = cr_ref[g][None], ci_ref[g][None]
        pr, pim = pr_ref[g, :ts], pi_ref[g, :ts]
        are = (cr * pr - ci * pim).reshape(ts * p, n)
        aim = (cr * pim + ci * pr).reshape(ts * p, n)
        k = (jnp.dot(are, br_ref[g], preferred_element_type=F32, precision=lax.Precision.HIGHEST)
             - jnp.dot(aim, bi_ref[g], preferred_element_type=F32, precision=lax.Precision.HIGHEST))
        k_ref[g] = k.reshape(ts, p, p)


def _s5_lag_kernels(cr, ci, bbr, bbi, pwr, pwi):
    g, p, n = cr.shape
    ts = pwr.shape[1] - 1
    gb = S5_LANE_GROUPS
    cspec = pl.BlockSpec((gb, p, n), lambda i: (i, 0, 0))
    bspec = pl.BlockSpec((gb, n, p), lambda i: (i, 0, 0))
    pspec = pl.BlockSpec((gb, ts + 1, 1, n), lambda i: (i, 0, 0, 0))
    return pl.pallas_call(
        _s5_lag_kernel,
        grid=(g // gb,),
        in_specs=[cspec, cspec, bspec, bspec, pspec, pspec],
        out_specs=pl.BlockSpec((gb, ts, p, p), lambda i: (i, 0, 0, 0)),
        out_shape=jax.ShapeDtypeStruct((g, ts, p, p), F32),
        compiler_params=_params("parallel"),
        name="s5_lag_kernels",
    )(cr, ci, bbr, bbi, pwr[:, :, None, :], pwi[:, :, None, :])


def _s5_operators(a_re, a_im, b_re, b_im, c_re, c_im, log_dt):
    ts, gl = S5_SUB, S5_LANE_GROUPS
    g, n = a_re.shape
    p = b_re.shape[-1]
    nc = g // gl
    dt = jnp.exp(log_dt)[:, None]
    lre, lim = dt * a_re, dt * a_im
    mag = jnp.exp(lre)
    abar_re, abar_im = mag * jnp.cos(lim), mag * jnp.sin(lim)
    den = a_re * a_re + a_im * a_im
    z_re = ((abar_re - 1.0) * a_re + abar_im * a_im) / den
    z_im = (abar_im * a_re - (abar_re - 1.0) * a_im) / den
    bb_re = z_re[..., None] * b_re - z_im[..., None] * b_im
    bb_im = z_re[..., None] * b_im + z_im[..., None] * b_re
    steps = jnp.arange(ts + 1, dtype=F32)[None, :, None]
    pmag = jnp.exp(steps * lre[:, None, :])
    pw_re = pmag * jnp.cos(steps * lim[:, None, :])
    pw_im = pmag * jnp.sin(steps * lim[:, None, :])

    k = _s5_lag_kernels(c_re, c_im, bb_re, bb_im, pw_re, pw_im)
    bd = jnp.swapaxes(k, 2, 3).reshape(nc, gl, ts, p, p).transpose(0, 2, 1, 3, 4).reshape(nc, ts, gl * p, p)

    rev_re, rev_im = pw_re[:, ts - 1::-1][:, :ts], pw_im[:, ts - 1::-1][:, :ts]
    bt_re, bt_im = jnp.swapaxes(bb_re, 1, 2), jnp.swapaxes(bb_im, 1, 2)
    sb_re = rev_re[:, :, None, :] * bt_re[:, None] - rev_im[:, :, None, :] * bt_im[:, None]
    sb_im = rev_re[:, :, None, :] * bt_im[:, None] + rev_im[:, :, None, :] * bt_re[:, None]

    def rows_sgq(a):
        a = a.reshape(nc, gl, ts, p, n).transpose(0, 2, 1, 3, 4).reshape(nc, ts * gl * p, n)
        return jnp.concatenate([a] * (LANES // n), axis=-1)
    s_re2, s_im2 = rows_sgq(sb_re), rows_sgq(sb_im)

    c4 = lambda a: a.reshape(nc, gl, p, n).transpose(0, 3, 1, 2)[:, :, None]
    p4 = lambda a: a[:, 1:].reshape(nc, gl, ts, n).transpose(0, 3, 2, 1)[..., None]
    ct_re = c4(c_re) * p4(pw_re) - c4(c_im) * p4(pw_im)
    ct_im = c4(c_re) * p4(pw_im) + c4(c_im) * p4(pw_re)
    cwc = jnp.stack([ct_re, -ct_im], axis=1).reshape(nc, 2, n, ts * gl * p)
    return bd, s_re2, s_im2, cwc, (lre, lim)


def _s5_wpow(lre, lim, n_pow):
    ts, gl = S5_SUB, S5_LANE_GROUPS
    g, n = lre.shape
    nc = g // gl
    e = (ts * (2.0 ** jnp.arange(n_pow, dtype=F32)))[None, :, None]
    mag = jnp.exp(e * lre[:, None, :])
    wr = (mag * jnp.cos(e * lim[:, None, :])).reshape(nc, gl, n_pow, n)
    wi = (mag * jnp.sin(e * lim[:, None, :])).reshape(nc, gl, n_pow, n)
    w = jnp.stack([wr, wi], axis=0)
    return jnp.transpose(w, (1, 3, 0, 2, 4)).reshape(nc, n_pow, 2 * gl * n)


def _s5_assemble(bd_ref, sre_ref, sim_ref, cw_ref, m_scr, s_scr, c_scr):
    ts = bd_ref.shape[1]
    n = cw_ref.shape[2]
    half = s_scr.shape[1] // 2
    gl = half // n
    p = LANES // gl
    square = (LANES, LANES)
    spread = (lax.broadcasted_iota(jnp.int32, (p, LANES), 1) % p == lax.broadcasted_iota(jnp.int32, (p, LANES), 0))
    spread = jnp.where(spread, 1.0, 0.0).astype(m_scr.dtype)
    same_group = lax.broadcasted_iota(jnp.int32, square, 0) // p == lax.broadcasted_iota(jnp.int32, square, 1) // p
    lag_blocks = [jnp.where(same_group, jnp.dot(bd_ref[0, j].astype(m_scr.dtype), spread, preferred_element_type=F32),
                            0.0).astype(m_scr.dtype) for j in range(ts)]
    for s in range(ts):
        for t in range(ts):
            blk = (slice(s * LANES, (s + 1) * LANES), slice(t * LANES, (t + 1) * LANES))
            if t >= s:
                m_scr[blk] = lag_blocks[t - s]
            elif t * LANES // MXU_DIM == s * LANES // MXU_DIM:
                m_scr[blk] = jnp.zeros((LANES, LANES), m_scr.dtype)
    shape = (s_scr.shape[0], half)
    own = ((lax.broadcasted_iota(jnp.int32, shape, 0) // p) % gl) == (lax.broadcasted_iota(jnp.int32, shape, 1) // n)
    for ref, lo in ((sre_ref, 0), (sim_ref, half)):
        wide = jnp.concatenate([ref[0]] * (half // LANES), axis=1)
        s_scr[:, lo:lo + half] = jnp.where(own, wide, 0.0).astype(s_scr.dtype)
    colg = (lax.broadcasted_iota(jnp.int32, (n, c_scr.shape[1]), 1) // p) % gl
    for r in range(2):
        for g in range(gl):
            c_scr[r * half + g * n:r * half + (g + 1) * n, :] = jnp.where(colg == g, cw_ref[0, r], 0.0).astype(c_scr.dtype)


def _s5_kernel(x_ref, bd_ref, sre_ref, sim_ref, cw_ref, w_ref, d_ref, y_ref, m_scr, s_scr, c_scr, *, n_pow):
    @pl.when(pl.program_id(1) == 0)
    def _():
        _s5_assemble(bd_ref, sre_ref, sim_ref, cw_ref, m_scr, s_scr, c_scr)

    x = x_ref[0]
    ds = jnp.dot(x, s_scr[...], preferred_element_type=F32)
    half = ds.shape[1] // 2
    rows = ds.shape[0]
    hr, hi = ds[:, :half], ds[:, half:]
    row = lax.broadcasted_iota(jnp.int32, hr.shape, 0)
    d = 1
    for k in range(n_pow):
        if d >= rows:
            break
        wr, wi = w_ref[0, k:k + 1, :half], w_ref[0, k:k + 1, half:]
        if d % SUBLANES == 0:
            sr, si = hr[:rows - d], hi[:rows - d]
            hr = jnp.concatenate([hr[:d], hr[d:] + (wr * sr - wi * si)], axis=0)
            hi = jnp.concatenate([hi[:d], hi[d:] + (wr * si + wi * sr)], axis=0)
        else:
            m = row >= d
            sr = jnp.where(m, pltpu.roll(hr, d, 0), 0.0)
            si = jnp.where(m, pltpu.roll(hi, d, 0), 0.0)
            hr, hi = hr + (wr * sr - wi * si), hi + (wr * si + wi * sr)
        d *= 2
    m = row >= 1
    hr = jnp.where(m, pltpu.roll(hr, 1, 0), 0.0)
    hi = jnp.where(m, pltpu.roll(hi, 1, 0), 0.0)
    hs = jnp.concatenate([hr, hi], axis=1).astype(BF16)
    for j in range(x.shape[1] // MXU_DIM):
        lo, hi_ = j * MXU_DIM, (j + 1) * MXU_DIM
        y = (jnp.dot(x[:, :hi_], m_scr[:hi_, lo:hi_], preferred_element_type=F32)
             + jnp.dot(hs, c_scr[:, lo:hi_], preferred_element_type=F32))
        y = y + d_ref[0, :, lo:hi_] * x[:, lo:hi_].astype(F32)
        y_ref[0, :, lo:hi_] = _gelu(y).astype(y_ref.dtype)


def _s5_branch(u2, layer, ops, bsz):
    nc, n_sub, width = u2.shape
    rows = n_sub // bsz
    cwc, wpow = ops[3], ops[4]
    n_state2 = 2 * S5_LANE_GROUPS * cwc.shape[3]
    x_spec = pl.BlockSpec((1, rows, width), lambda c, b: (c, b, 0))
    op_spec = lambda a: pl.BlockSpec((None, 1) + a.shape[2:], lambda c, b: (layer, c) + (0,) * (a.ndim - 2))
    return pl.pallas_call(
        functools.partial(_s5_kernel, n_pow=wpow.shape[2]),
        grid=(nc, bsz),
        in_specs=[x_spec] + [op_spec(a) for a in ops],
        out_specs=x_spec,
        out_shape=jax.ShapeDtypeStruct(u2.shape, BF16),
        scratch_shapes=[pltpu.VMEM((width, width), BF16), pltpu.VMEM((width, n_state2), BF16),
                        pltpu.VMEM((n_state2, width), BF16)],
        compiler_params=_params("arbitrary", "arbitrary"),
        name="s5_branch",
    )(u2, *ops)


def _route(logits_t, bias):
    e, tm = logits_t.shape
    per = e // N_EXPERT_GROUPS
    scores = _sigmoid(logits_t)
    biased = scores + bias
    g3 = biased.reshape(N_EXPERT_GROUPS, per, tm)
    sub = lax.broadcasted_iota(jnp.int32, g3.shape, 1)
    m1 = jnp.max(g3, axis=1, keepdims=True)
    first = jnp.min(jnp.where(g3 == m1, sub, per), axis=1, keepdims=True)
    m2 = jnp.max(jnp.where(sub == first, -jnp.inf, g3), axis=1, keepdims=True)
    gs = (m1 + m2).reshape(N_EXPERT_GROUPS, tm)
    gidx = lax.broadcasted_iota(jnp.int32, gs.shape, 0)
    grank = jnp.zeros(gs.shape, jnp.int32)
    for j in range(N_EXPERT_GROUPS):
        other = gs[j:j + 1, :]
        ahead = jnp.logical_or(other > gs, jnp.logical_and(other == gs, j < gidx))
        grank = grank + ahead.astype(jnp.int32)
    gsel = (grank < TOPK_GROUPS).reshape(N_EXPERT_GROUPS, 1, tm)
    masked = jnp.where(gsel, g3, -jnp.inf).reshape(e, tm)
    eidx = lax.broadcasted_iota(jnp.int32, masked.shape, 0)
    erank = jnp.zeros(masked.shape, jnp.int32)
    for j in range(e):
        other = masked[j:j + 1, :]
        ahead = jnp.logical_or(other > masked, jnp.logical_and(other == masked, j < eidx))
        erank = erank + ahead.astype(jnp.int32)
    sel = erank < TOP_K
    gate = jnp.where(sel, scores, 0.0)
    denom = jnp.sum(gate, axis=0, keepdims=True)
    return ROUTED_SCALE * gate / denom


def _mixout_kernel(ys_ref, ma_ref, gb_ref, x_ref, wglu_ref, bglu_ref, wout_ref, g_ref, b_ref, rw_ref, rb_ref,
                   x1_ref, comb_ref, ys_scr, *, alpha):
    d = x_ref.shape[1]
    nc, n_sub, width = ys_ref.shape
    ts = width // LANES
    for c in range(nc):
        for s in range(ts):
            ys_scr[c, pl.ds(s, n_sub, stride=ts), :] = ys_ref[c, :, s * LANES:(s + 1) * LANES].astype(F32)
    ys = jnp.concatenate([ys_scr[c] for c in range(nc)], axis=1).astype(BF16)
    glu = jnp.dot(ys, wglu_ref[...], preferred_element_type=F32) + bglu_ref[...]
    yb = glu[:, :d] * _sigmoid(glu[:, d:])
    merged = ma_ref[...].astype(F32) + _sigmoid(gb_ref[...].astype(F32)) * yb
    mix = jnp.dot(merged.astype(BF16), wout_ref[...], preferred_element_type=F32)
    x1 = _layer_norm(alpha * x_ref[...] + mix, g_ref[...], b_ref[...])
    x1_ref[...] = x1
    rw = rw_ref[...]
    rw_hi = rw.astype(BF16)
    rw_lo = (rw - rw_hi.astype(F32)).astype(BF16)
    x_hi = x1.astype(BF16)
    x_lo = (x1 - x_hi.astype(F32)).astype(BF16)
    nt = (((1,), (1,)), ((), ()))
    logits_t = (lax.dot_general(rw_hi, x_hi, nt, preferred_element_type=F32)
                + lax.dot_general(rw_hi, x_lo, nt, preferred_element_type=F32)
                + lax.dot_general(rw_lo, x_hi, nt, preferred_element_type=F32))
    comb_ref[...] = _route(logits_t, rb_ref[...]).T


def _mixout(ys, ma, gb, x, layer, params, alpha, tm):
    t, d = x.shape
    e = params[5].shape[1]
    nc, _, width = ys.shape
    ts = width // LANES
    row = lambda i: (i, 0)
    return pl.pallas_call(
        functools.partial(_mixout_kernel, alpha=alpha),
        grid=(t // tm,),
        in_specs=[pl.BlockSpec((nc, tm // ts, width), lambda i: (0, i, 0))] +
                 [pl.BlockSpec((tm, d), row)] * 3 + [_layer_spec(a, layer) for a in params],
        out_specs=[pl.BlockSpec((tm, d), row), pl.BlockSpec((tm, e), row)],
        out_shape=[jax.ShapeDtypeStruct((t, d), F32), jax.ShapeDtypeStruct((t, e), F32)],
        scratch_shapes=[pltpu.VMEM((nc, tm, LANES), F32)],
        compiler_params=_params("parallel"),
        name="mixout",
    )(ys, ma, gb, x, *params)


def _moe_kernel(x_ref, comb_ref, wg_ref, wu_ref, wd_ref, sg_ref, su_ref, sd_ref, p_ref, pw_ref, pgw_ref, pgb_ref,
                g_ref, b_ref, o_ref, xb_ref, acc_ref, *, alpha):
    c = pl.program_id(1)
    ec = wg_ref.shape[0]

    def ffn(xb, wg, wu, wd, scale):
        gg = jnp.dot(xb, wg, preferred_element_type=F32)
        hdn = gg * _sigmoid(gg) * jnp.dot(xb, wu, preferred_element_type=F32)
        if scale is not None:
            hdn = hdn * scale
        return jnp.dot(hdn.astype(BF16), wd, preferred_element_type=F32)

    @pl.when(c == 0)
    def _():
        x = x_ref[...]
        xb = x.astype(BF16)
        xb_ref[...] = xb
        gate = _sigmoid(jnp.dot(xb, pgw_ref[...], preferred_element_type=F32) + pgb_ref[...])
        ple = gate * jnp.dot(p_ref[...].astype(BF16), pw_ref[...], preferred_element_type=F32)
        acc_ref[...] = alpha * x + ple + ffn(xb, sg_ref[...], su_ref[...], sd_ref[...], None)

    xb = xb_ref[...]
    comb = comb_ref[...]
    lane = lax.broadcasted_iota(jnp.int32, comb.shape, 1)
    y = None
    for e in range(ec):
        col = jnp.sum(jnp.where(lane == c * ec + e, comb, 0.0), axis=1, keepdims=True)
        ye = ffn(xb, wg_ref[e], wu_ref[e], wd_ref[e], col)
        y = ye if y is None else y + ye
    acc_ref[...] += y

    @pl.when(c == pl.num_programs(1) - 1)
    def _():
        o_ref[...] = _layer_norm(acc_ref[...], g_ref[...], b_ref[...])


def _moe(x1, comb, layer, wg, wu, wd, sg, su, sd, p, pw, pgw, pgb, g, b, alpha, tm, ec):
    t, d = x1.shape
    _, e, _, f = wg.shape
    dp = p.shape[2]
    expert = lambda i, c: (layer, c, 0, 0)
    return pl.pallas_call(
        functools.partial(_moe_kernel, alpha=alpha),
        grid=(t // tm, e // ec),
        in_specs=[pl.BlockSpec((tm, d), lambda i, c: (i, 0)),
                  pl.BlockSpec((tm, e), lambda i, c: (i, 0)),
                  pl.BlockSpec((None, ec, d, f), expert),
                  pl.BlockSpec((None, ec, d, f), expert),
                  pl.BlockSpec((None, ec, f, d), expert),
                  _layer_spec(sg, layer), _layer_spec(su, layer), _layer_spec(sd, layer),
                  pl.BlockSpec((None, tm, dp), lambda i, c: (layer, i, 0)),
                  _layer_spec(pw, layer), _layer_spec(pgw, layer), _layer_spec(pgb, layer),
                  _layer_spec(g, layer), _layer_spec(b, layer)],
        out_specs=pl.BlockSpec((tm, d), lambda i, c: (i, 0)),
        out_shape=jax.ShapeDtypeStruct((t, d), F32),
        scratch_shapes=[pltpu.VMEM((tm, d), BF16), pltpu.VMEM((tm, d), F32)],
        compiler_params=_params("parallel", "arbitrary"),
        name="moe",
    )(x1, comb, wg, wu, wd, sg, su, sd, p, pw, pgw, pgb, g, b)


def _pick_tile(n, target):
    tm = min(n, target)
    assert n % tm == 0, (n, tm)
    return tm


def kernel(x, p, w_in, conv_w, conv_b, lru_wa, lru_ba, lru_wx, lru_bx, lru_lambda, w_lru_out, s5_a_re, s5_a_im, s5_b_re, s5_b_im, s5_c_re, s5_c_im, s5_d, s5_log_dt, w_glu, b_glu, w_out, ln1_g, ln1_b, router_w, router_bias, moe_w_gate, moe_w_up, moe_w_down, shared_w_gate, shared_w_up, shared_w_down, ple_w, ple_gate_w, ple_gate_b, ln2_g, ln2_b):
    depth = w_in.shape[0]
    bsz, seq, d = x.shape
    t = bsz * seq
    d_rnn = conv_w.shape[-1]
    d_s5 = s5_d.shape[-1]
    alpha = (2.0 * depth) ** 0.25
    widths = (d_rnn, d_rnn, d_s5, d, d)
    assert sum(widths) == w_in.shape[-1]
    assert d_s5 % LANES == 0 and seq % S5_SUB == 0

    heads, dh = lru_wa.shape[1], lru_wa.shape[2]
    hpb = (LANES // math.gcd(dh, LANES))
    assert heads % hpb == 0
    ec = 4
    tm_lru = _pick_tile(seq, 256)
    tm_mix = _pick_tile(t, 512)
    tm_moe = _pick_tile(t, 1024)
    n_pow = max(1, int(math.ceil(math.log2(seq // S5_SUB))))

    bf = lambda a: a.astype(BF16)
    w_in_b, w_lru_out_b, w_glu_b, w_out_b = bf(w_in), bf(w_lru_out), bf(w_glu), bf(w_out)
    wg_b, wu_b, wd_b = bf(moe_w_gate), bf(moe_w_up), bf(moe_w_down)
    sg_b, su_b, sd_b = bf(shared_w_gate), bf(shared_w_up), bf(shared_w_down)
    ple_w_b, ple_gate_w_b = bf(ple_w), bf(ple_gate_w)
    lru_params = (conv_w, _row3(conv_b), _block_diag_heads(lru_wa, hpb), _row3(lru_ba.reshape(depth, -1)),
                  _block_diag_heads(lru_wx, hpb), _row3(lru_bx.reshape(depth, -1)), _row3(lru_lambda), w_lru_out_b)
    mix_params = (w_glu_b, _row3(b_glu), w_out_b, _row3(ln1_g), _row3(ln1_b),
                  jnp.swapaxes(router_w, 1, 2), router_bias[..., None])

    groups, n_state = s5_a_re.shape[1], s5_a_re.shape[2]
    nc = d_s5 // LANES
    fold = lambda a: a.reshape((depth * groups,) + a.shape[2:])
    bd, s_re2, s_im2, cwc, (lre, lim) = _s5_operators(fold(s5_a_re), fold(s5_a_im), fold(s5_b_re), fold(s5_b_im),
                                                     fold(s5_c_re), fold(s5_c_im), fold(s5_log_dt))
    unfold = lambda a: a.reshape((depth, nc) + a.shape[1:])
    d_t = jnp.tile(s5_d.reshape(depth, nc, 1, LANES), (1, 1, 1, S5_SUB))
    s5_ops = (unfold(bd), unfold(s_re2), unfold(s_im2), unfold(cwc), unfold(_s5_wpow(lre, lim, n_pow)), d_t)
    assert groups == nc * S5_LANE_GROUPS and n_state * (LANES // n_state) == LANES

    h = x.reshape(t, d)
    p2 = p.reshape(depth, t, p.shape[-1])
    for i in range(depth):
        u_s5, gate_b, m_a = _inproj_lru(h, i, w_in_b, lru_params, widths, S5_SUB, bsz, tm_lru)
        ys = _s5_branch(u_s5, i, s5_ops, bsz)
        x1, comb = _mixout(ys, m_a, gate_b, h, i, mix_params, alpha, tm_mix)
        h = _moe(x1, comb, i, wg_b, wu_b, wd_b, sg_b, su_b, sd_b, p2, ple_w_b, ple_gate_w_b, _row3(ple_gate_b),
                 _row3(ln2_g), _row3(ln2_b), alpha, tm_moe, ec)
    return h.reshape(bsz, seq, d)
```

```python
import functools
import math

import jax
import jax.numpy as jnp
from jax import lax
from jax.experimental import pallas as pl
from jax.experimental.pallas import tpu as pltpu

F32 = jnp.float32
BF16 = jnp.bfloat16

LRU_C = 8.0
TOP_K = 8
N_EXPERT_GROUPS = 8
TOPK_GROUPS = 4
ROUTED_SCALE = 2.5
LN_EPS = 1e-5

LANES = 128
SUBLANES = 8
MXU_DIM = 256
PROJ_CHUNK = 2 * MXU_DIM
VMEM_LIMIT_BYTES = 60 * 1024 * 1024

S5_SUB = 16
S5_LANE_GROUPS = 8


def _params(*sem):
    return pltpu.CompilerParams(dimension_semantics=sem, vmem_limit_bytes=VMEM_LIMIT_BYTES)


def _layer_spec(stacked, layer, single_buffer=False):
    nd = stacked.ndim - 1
    mode = {"pipeline_mode": pl.Buffered(1)} if single_buffer else {}
    return pl.BlockSpec((None,) + stacked.shape[1:], lambda *_: (layer,) + (0,) * nd, **mode)


def _row3(v):
    return v.reshape(v.shape[0], 1, -1)


def _gelu(x):
    return 0.5 * x * (1.0 + jnp.tanh(math.sqrt(2.0 / math.pi) * (x + 0.044715 * (x * x * x))))


def _sigmoid(x):
    return 1.0 / (1.0 + jnp.exp(-x))


def _layer_norm(v, g, b):
    mu = jnp.mean(v, axis=-1, keepdims=True)
    c = v - mu
    var = jnp.mean(c * c, axis=-1, keepdims=True)
    return c * lax.rsqrt(var + LN_EPS) * g + b


def _shift_rows(v, d, fill):
    n = v.shape[0]
    if d % SUBLANES == 0:
        head = jnp.broadcast_to(jnp.asarray(fill, v.dtype), (d,) + v.shape[1:])
        return jnp.concatenate([head, v[:n - d]], axis=0)
    row = lax.broadcasted_iota(jnp.int32, v.shape, 0)
    return jnp.where(row >= d, pltpu.roll(v, d, 0), fill)


def _scan_rows(a, b, h0, a_scr, b_scr, h_scr):
    n, c = a.shape
    ng = n // SUBLANES
    for ct in range(c // LANES):
        a_scr[ct] = a[:, ct * LANES:(ct + 1) * LANES]
        b_scr[ct] = b[:, ct * LANES:(ct + 1) * LANES]
    for ct in range(c // LANES):
        every = lambda s: pl.ds(s, ng, stride=SUBLANES)
        a_s = [a_scr[ct, every(s), :] for s in range(SUBLANES)]
        acum, hloc = [a_s[0]], [b_scr[ct, every(0), :]]
        for s in range(1, SUBLANES):
            hloc.append(a_s[s] * hloc[-1] + b_scr[ct, every(s), :])
            acum.append(a_s[s] * acum[-1])
        ga, gh = acum[-1], hloc[-1]
        d = 1
        while d < ng:
            gh = gh + ga * _shift_rows(gh, d, 0.0)
            ga = ga * _shift_rows(ga, d, 1.0)
            d *= 2
        h0c = h0[:, ct * LANES:(ct + 1) * LANES]
        after = gh + ga * h0c
        cin = _shift_rows(after, 1, h0c)
        for s in range(SUBLANES):
            h_scr[ct, every(s), :] = hloc[s] + acum[s] * cin
    return jnp.concatenate([h_scr[ct] for ct in range(c // LANES)], axis=1)


def _lru_tile(x_ref, g_ref, ga_ref, first, cw_ref, cb_ref, wa_ref, ba_ref, wx_ref, bx_ref, lam_ref, wo_ref,
              prev_ref, h_ref, a_scr, b_scr, hs_scr, n_blk, blk, side_work):
    sites = 5
    per_phase = -(-len(side_work) // sites)

    def run_side_work():
        for _ in range(min(per_phase, len(side_work))):
            side_work.pop(0)()

    tm = x_ref.shape[0]
    run_side_work()
    x = x_ref[...].astype(F32)
    ext = jnp.concatenate([prev_ref[...], x], axis=0)
    prev_ref[...] = x[tm - SUBLANES:, :]
    kw = cw_ref.shape[0]
    xc = cb_ref[...] + cw_ref[kw - 1:kw, :] * x
    for j in range(1, kw):
        xc = xc + cw_ref[kw - 1 - j:kw - j, :] * pltpu.roll(ext, j, 0)[SUBLANES:, :]

    run_side_work()
    xcb = xc.astype(BF16)
    r_parts, i_parts = [], []
    for k in range(n_blk):
        xk = xcb[:, k * blk:(k + 1) * blk]
        r_parts.append(jnp.dot(xk, wa_ref[k], preferred_element_type=F32))
        i_parts.append(jnp.dot(xk, wx_ref[k], preferred_element_type=F32))
    r = _sigmoid(jnp.concatenate(r_parts, axis=1) + ba_ref[...])
    ig = _sigmoid(jnp.concatenate(i_parts, axis=1) + bx_ref[...])

    run_side_work()
    nl = -lam_ref[...]
    softplus = jnp.maximum(nl, 0.0) + jnp.log1p(jnp.exp(-jnp.abs(nl)))
    log_a = (-LRU_C) * r * softplus
    a = jnp.exp(log_a)
    v = 1.0 - a * a
    mult = jnp.where(v > 0.0, v * lax.rsqrt(v), 0.0)
    row = lax.broadcasted_iota(jnp.int32, a.shape, 0)
    mult = jnp.where(jnp.logical_and(row == 0, first), 1.0, mult)
    b = mult * (ig * xc)

    run_side_work()
    h = _scan_rows(a, b, h_ref[0:1, :], a_scr, b_scr, hs_scr)
    h_ref[...] = jnp.broadcast_to(h[tm - 1:tm, :], h_ref.shape)

    run_side_work()
    y = (_gelu(g_ref[...].astype(F32)) * h).astype(BF16)
    ya = jnp.dot(y, wo_ref[...], preferred_element_type=F32)
    out = _sigmoid(ga_ref[...].astype(F32)) * ya
    while side_work:
        side_work.pop(0)()
    return out


def _inlru_kernel(x_ref, w_ref, cw_ref, cb_ref, wa_ref, ba_ref, wx_ref, bx_ref, lam_ref, wo_ref,
                  u_ref, gb_ref, ma_ref,
                  xl_scr, gl_scr, ga_scr, z_scr, prev_ref, h_ref, a_scr, b_scr, hs_scr, *, bounds, n_blk, blk):
    j = pl.program_id(1)

    @pl.when(j == 0)
    def _():
        xl_scr[...] = jnp.zeros_like(xl_scr)
        gl_scr[...] = jnp.zeros_like(gl_scr)
        ga_scr[...] = jnp.zeros_like(ga_scr)

    @pl.when(j <= 1)
    def _():
        prev_ref[...] = jnp.zeros_like(prev_ref)
        h_ref[...] = jnp.zeros_like(h_ref)

    fill, drain = j % 2, (j + 1) % 2
    xb = x_ref[...].astype(BF16)
    ts = u_ref.shape[2] // LANES

    def project(lo, hi, store):
        def thunk():
            store(jnp.dot(xb, w_ref[:, lo:hi], preferred_element_type=F32))
        return thunk

    def to_stash(scr, off):
        def store(z):
            scr[fill, :, off:off + z.shape[1]] = z.astype(scr.dtype)
        return store

    def to_gate_b(off):
        def store(z):
            gb_ref[:, off:off + z.shape[1]] = z.astype(gb_ref.dtype)
        return store

    def to_s5(off):
        def store(z):
            for k in range(z.shape[1] // LANES):
                c = off // LANES + k
                z_scr[c] = z[:, k * LANES:(k + 1) * LANES]
                for s in range(ts):
                    u_ref[c, :, s * LANES:(s + 1) * LANES] = z_scr[
                        c, pl.ds(s, u_ref.shape[1], stride=ts), :].astype(u_ref.dtype)
        return store

    sinks = (functools.partial(to_stash, xl_scr), functools.partial(to_stash, gl_scr), to_s5,
             functools.partial(to_stash, ga_scr), to_gate_b)
    side_work = []
    for (lo, hi), sink in zip(bounds, sinks):
        for c0 in range(lo, hi, PROJ_CHUNK):
            side_work.append(project(c0, min(c0 + PROJ_CHUNK, hi), sink(c0 - lo)))

    ma = _lru_tile(xl_scr.at[drain], gl_scr.at[drain], ga_scr.at[drain], j == 1,
                   cw_ref, cb_ref, wa_ref, ba_ref, wx_ref, bx_ref, lam_ref, wo_ref,
                   prev_ref, h_ref, a_scr, b_scr, hs_scr, n_blk, blk, side_work)
    ma_ref[...] = ma.astype(ma_ref.dtype)


def _inproj_lru(x, layer, w_all, lru_params, widths, ts, bsz, tm):
    t, d = x.shape
    d_rnn, _, d_s5, d_a, d_b = widths
    seq = t // bsz
    nt = seq // tm
    _, n_blk, blk, _ = lru_params[2].shape
    bounds, lo = [], 0
    for w in widths:
        bounds.append((lo, lo + w))
        lo += w
    cur = lambda b, j: (b * nt + jnp.minimum(j, nt - 1), 0)
    prv = lambda b, j: (b * nt + jnp.maximum(j - 1, 0), 0)
    nc = d_s5 // LANES
    return pl.pallas_call(
        functools.partial(_inlru_kernel, bounds=tuple(bounds), n_blk=n_blk, blk=blk),
        grid=(bsz, nt + 1),
        in_specs=[pl.BlockSpec((tm, d), cur), _layer_spec(w_all, layer, single_buffer=True)]
                 + [_layer_spec(a, layer) for a in lru_params],
        out_specs=[pl.BlockSpec((nc, tm // ts, ts * LANES), lambda b, j: (0,) + cur(b, j)),
                   pl.BlockSpec((tm, d_b), cur), pl.BlockSpec((tm, d_a), prv)],
        out_shape=[jax.ShapeDtypeStruct((nc, t // ts, ts * LANES), BF16),
                   jax.ShapeDtypeStruct((t, d_b), BF16), jax.ShapeDtypeStruct((t, d_a), BF16)],
        scratch_shapes=[pltpu.VMEM((2, tm, d_rnn), BF16), pltpu.VMEM((2, tm, d_rnn), BF16),
                        pltpu.VMEM((2, tm, d_a), BF16),
                        pltpu.VMEM((nc, tm, LANES), F32),
                        pltpu.VMEM((SUBLANES, d_rnn), F32), pltpu.VMEM((SUBLANES, d_rnn), F32)]
                       + [pltpu.VMEM((d_rnn // LANES, tm, LANES), F32)] * 3,
        compiler_params=_params("arbitrary", "arbitrary"),
        name="inproj_lru",
    )(x, w_all, *lru_params)


def _block_diag_heads(w, heads_per_blk):
    l, h, dh, _ = w.shape
    nb = h // heads_per_blk
    eye = jnp.eye(heads_per_blk, dtype=w.dtype)
    wb = w.reshape(l * nb, heads_per_blk, dh, dh)
    out = jnp.einsum("nhij,hk->nhikj", wb, eye)
    return out.reshape(l, nb, heads_per_blk * dh, heads_per_blk * dh).astype(BF16)


def _s5_lag_kernel(cr_ref, ci_ref, br_ref, bi_ref, pr_ref, pi_ref, k_ref):
    gb, p, n = cr_ref.shape
    ts = k_ref.shape[1]
    for g in range(gb):
        cr, ci = cr_ref[g][None], ci_ref[g][None]
        pr, pim = pr_ref[g, :ts], pi_ref[g, :ts]
        are = (cr * pr - ci * pim).reshape(ts * p, n)
        aim = (cr * pim + ci * pr).reshape(ts * p, n)
        k = (jnp.dot(are, br_ref[g], preferred_element_type=F32, precision=lax.Precision.HIGHEST)
             - jnp.dot(aim, bi_ref[g], preferred_element_type=F32, precision=lax.Precision.HIGHEST))
        k_ref[g] = k.reshape(ts, p, p)


def _s5_lag_kernels(cr, ci, bbr, bbi, pwr, pwi):
    g, p, n = cr.shape
    ts = pwr.shape[1] - 1
    gb = S5_LANE_GROUPS
    cspec = pl.BlockSpec((gb, p, n), lambda i: (i, 0, 0))
    bspec = pl.BlockSpec((gb, n, p), lambda i: (i, 0, 0))
    pspec = pl.BlockSpec((gb, ts + 1, 1, n), lambda i: (i, 0, 0, 0))
    return pl.pallas_call(
        _s5_lag_kernel,
        grid=(g // gb,),
        in_specs=[cspec, cspec, bspec, bspec, pspec, pspec],
        out_specs=pl.BlockSpec((gb, ts, p, p), lambda i: (i, 0, 0, 0)),
        out_shape=jax.ShapeDtypeStruct((g, ts, p, p), F32),
        compiler_params=_params("parallel"),
        name="s5_lag_kernels",
    )(cr, ci, bbr, bbi, pwr[:, :, None, :], pwi[:, :, None, :])


def _s5_operators(a_re, a_im, b_re, b_im, c_re, c_im, log_dt):
    ts, gl = S5_SUB, S5_LANE_GROUPS
    g, n = a_re.shape
    p = b_re.shape[-1]
    nc = g // gl
    dt = jnp.exp(log_dt)[:, None]
    lre, lim = dt * a_re, dt * a_im
    mag = jnp.exp(lre)
    abar_re, abar_im = mag * jnp.cos(lim), mag * jnp.sin(lim)
    den = a_re * a_re + a_im * a_im
    z_re = ((abar_re - 1.0) * a_re + abar_im * a_im) / den
    z_im = (abar_im * a_re - (abar_re - 1.0) * a_im) / den
    bb_re = z_re[..., None] * b_re - z_im[..., None] * b_im
    bb_im = z_re[..., None] * b_im + z_im[..., None] * b_re
    steps = jnp.arange(ts + 1, dtype=F32)[None, :, None]
    pmag = jnp.exp(steps * lre[:, None, :])
    pw_re = pmag * jnp.cos(steps * lim[:, None, :])
    pw_im = pmag * jnp.sin(steps * lim[:, None, :])

    k = _s5_lag_kernels(c_re, c_im, bb_re, bb_im, pw_re, pw_im)
    bd = jnp.swapaxes(k, 2, 3).reshape(nc, gl, ts, p, p).transpose(0, 2, 1, 3, 4).reshape(nc, ts, gl * p, p)

    rev_re, rev_im = pw_re[:, ts - 1::-1][:, :ts], pw_im[:, ts - 1::-1][:, :ts]
    bt_re, bt_im = jnp.swapaxes(bb_re, 1, 2), jnp.swapaxes(bb_im, 1, 2)
    sb_re = rev_re[:, :, None, :] * bt_re[:, None] - rev_im[:, :, None, :] * bt_im[:, None]
    sb_im = rev_re[:, :, None, :] * bt_im[:, None] + rev_im[:, :, None, :] * bt_re[:, None]

    def rows_sgq(a):
        a = a.reshape(nc, gl, ts, p, n).transpose(0, 2, 1, 3, 4).reshape(nc, ts * gl * p, n)
        return jnp.concatenate([a] * (LANES // n), axis=-1)
    s_re2, s_im2 = rows_sgq(sb_re), rows_sgq(sb_im)

    c4 = lambda a: a.reshape(nc, gl, p, n).transpose(0, 3, 1, 2)[:, :, None]
    p4 = lambda a: a[:, 1:].reshape(nc, gl, ts, n).transpose(0, 3, 2, 1)[..., None]
    ct_re = c4(c_re) * p4(pw_re) - c4(c_im) * p4(pw_im)
    ct_im = c4(c_re) * p4(pw_im) + c4(c_im) * p4(pw_re)
    cwc = jnp.stack([ct_re, -ct_im], axis=1).reshape(nc, 2, n, ts * gl * p)
    return bd, s_re2, s_im2, cwc, (lre, lim)


def _s5_wpow(lre, lim, n_pow):
    ts, gl = S5_SUB, S5_LANE_GROUPS
    g, n = lre.shape
    nc = g // gl
    e = (ts * (2.0 ** jnp.arange(n_pow, dtype=F32)))[None, :, None]
    mag = jnp.exp(e * lre[:, None, :])
    wr = (mag * jnp.cos(e * lim[:, None, :])).reshape(nc, gl, n_pow, n)
    wi = (mag * jnp.sin(e * lim[:, None, :])).reshape(nc, gl, n_pow, n)
    w = jnp.stack([wr, wi], axis=0)
    return jnp.transpose(w, (1, 3, 0, 2, 4)).reshape(nc, n_pow, 2 * gl * n)


def _s5_assemble(bd_ref, sre_ref, sim_ref, cw_ref, m_scr, s_scr, c_scr):
    ts = bd_ref.shape[1]
    n = cw_ref.shape[2]
    half = s_scr.shape[1] // 2
    gl = half // n
    p = LANES // gl
    square = (LANES, LANES)
    spread = (lax.broadcasted_iota(jnp.int32, (p, LANES), 1) % p == lax.broadcasted_iota(jnp.int32, (p, LANES), 0))
    spread = jnp.where(spread, 1.0, 0.0).astype(m_scr.dtype)
    same_group = lax.broadcasted_iota(jnp.int32, square, 0) // p == lax.broadcasted_iota(jnp.int32, square, 1) // p
    lag_blocks = [jnp.where(same_group, jnp.dot(bd_ref[0, j].astype(m_scr.dtype), spread, preferred_element_type=F32),
                            0.0).astype(m_scr.dtype) for j in range(ts)]
    for s in range(ts):
        for t in range(ts):
            blk = (slice(s * LANES, (s + 1) * LANES), slice(t * LANES, (t + 1) * LANES))
            if t >= s:
                m_scr[blk] = lag_blocks[t - s]
            elif t * LANES // MXU_DIM == s * LANES // MXU_DIM:
                m_scr[blk] = jnp.zeros((LANES, LANES), m_scr.dtype)
    shape = (s_scr.shape[0], half)
    own = ((lax.broadcasted_iota(jnp.int32, shape, 0) // p) % gl) == (lax.broadcasted_iota(jnp.int32, shape, 1) // n)
    for ref, lo in ((sre_ref, 0), (sim_ref, half)):
        wide = jnp.concatenate([ref[0]] * (half // LANES), axis=1)
        s_scr[:, lo:lo + half] = jnp.where(own, wide, 0.0).astype(s_scr.dtype)
    colg = (lax.broadcasted_iota(jnp.int32, (n, c_scr.shape[1]), 1) // p) % gl
    for r in range(2):
        for g in range(gl):
            c_scr[r * half + g * n:r * half + (g + 1) * n, :] = jnp.where(colg == g, cw_ref[0, r], 0.0).astype(c_scr.dtype)


def _s5_state_scan(ds, w_ref, ds_scr, hs_scr, n_pow):
    rows, width = ds.shape
    half = width // 2
    nct = half // LANES
    ng = rows // SUBLANES
    assert rows <= 2 ** n_pow and rows % SUBLANES == 0
    for ct in range(2 * nct):
        ds_scr[ct] = ds[:, ct * LANES:(ct + 1) * LANES]
    every = lambda s: pl.ds(s, ng, stride=SUBLANES)
    cmul = lambda ar, ai, br, bi: (ar * br - ai * bi, ar * bi + ai * br)
    for ct in range(nct):
        cols = slice(ct * LANES, (ct + 1) * LANES)
        icols = slice(half + ct * LANES, half + (ct + 1) * LANES)
        w1r, w1i = w_ref[0, 0:1, cols], w_ref[0, 0:1, icols]
        hloc = [(ds_scr[ct, every(0), :], ds_scr[nct + ct, every(0), :])]
        for s in range(1, SUBLANES):
            pr, pi = cmul(w1r, w1i, *hloc[-1])
            hloc.append((pr + ds_scr[ct, every(s), :], pi + ds_scr[nct + ct, every(s), :]))
        gr, gi = hloc[-1]
        d, k = 1, 3
        while d < ng:
            sr, si = cmul(w_ref[0, k:k + 1, cols], w_ref[0, k:k + 1, icols],
                          _shift_rows(gr, d, 0.0), _shift_rows(gi, d, 0.0))
            gr, gi = gr + sr, gi + si
            d, k = 2 * d, k + 1
        cr, ci = _shift_rows(gr, 1, 0.0), _shift_rows(gi, 1, 0.0)
        hs_scr[ct, every(0), :] = cr
        hs_scr[nct + ct, every(0), :] = ci
        pr, pi = w1r, w1i
        for s in range(1, SUBLANES):
            ar, ai = cmul(pr, pi, cr, ci)
            hs_scr[ct, every(s), :] = hloc[s - 1][0] + ar
            hs_scr[nct + ct, every(s), :] = hloc[s - 1][1] + ai
            if s + 1 < SUBLANES:
                pr, pi = cmul(pr, pi, w1r, w1i)
    return jnp.concatenate([hs_scr[ct] for ct in range(2 * nct)], axis=1)


def _s5_kernel(x_ref, bd_ref, sre_ref, sim_ref, cw_ref, w_ref, d_ref, y_ref, m_scr, s_scr, c_scr, ds_scr, hs_scr,
               *, n_pow):
    @pl.when(pl.program_id(1) == 0)
    def _():
        _s5_assemble(bd_ref, sre_ref, sim_ref, cw_ref, m_scr, s_scr, c_scr)

    x = x_ref[0]
    ds = jnp.dot(x, s_scr[...], preferred_element_type=F32)
    hs = _s5_state_scan(ds, w_ref, ds_scr, hs_scr, n_pow).astype(BF16)
    for j in range(x.shape[1] // MXU_DIM):
        lo, hi_ = j * MXU_DIM, (j + 1) * MXU_DIM
        y = (jnp.dot(x[:, :hi_], m_scr[:hi_, lo:hi_], preferred_element_type=F32)
             + jnp.dot(hs, c_scr[:, lo:hi_], preferred_element_type=F32))
        y = y + d_ref[0, :, lo:hi_] * x[:, lo:hi_].astype(F32)
        y_ref[0, :, lo:hi_] = _gelu(y).astype(y_ref.dtype)


def _s5_branch(u2, layer, ops, bsz):
    nc, n_sub, width = u2.shape
    rows = n_sub // bsz
    cwc, wpow = ops[3], ops[4]
    n_state2 = 2 * S5_LANE_GROUPS * cwc.shape[3]
    x_spec = pl.BlockSpec((1, rows, width), lambda c, b: (c, b, 0))
    op_spec = lambda a: pl.BlockSpec((None, 1) + a.shape[2:], lambda c, b: (layer, c) + (0,) * (a.ndim - 2))
    return pl.pallas_call(
        functools.partial(_s5_kernel, n_pow=wpow.shape[2]),
        grid=(nc, bsz),
        in_specs=[x_spec] + [op_spec(a) for a in ops],
        out_specs=x_spec,
        out_shape=jax.ShapeDtypeStruct(u2.shape, BF16),
        scratch_shapes=[pltpu.VMEM((width, width), BF16), pltpu.VMEM((width, n_state2), BF16),
                        pltpu.VMEM((n_state2, width), BF16)]
                       + [pltpu.VMEM((n_state2 // LANES, rows, LANES), F32)] * 2,
        compiler_params=_params("arbitrary", "arbitrary"),
        name="s5_branch",
    )(u2, *ops)


def _route(logits_t, bias):
    e, tm = logits_t.shape
    per = e // N_EXPERT_GROUPS
    scores = _sigmoid(logits_t)
    biased = scores + bias
    g3 = biased.reshape(N_EXPERT_GROUPS, per, tm)
    sub = lax.broadcasted_iota(jnp.int32, g3.shape, 1)
    m1 = jnp.max(g3, axis=1, keepdims=True)
    first = jnp.min(jnp.where(g3 == m1, sub, per), axis=1, keepdims=True)
    m2 = jnp.max(jnp.where(sub == first, -jnp.inf, g3), axis=1, keepdims=True)
    gs = (m1 + m2).reshape(N_EXPERT_GROUPS, tm)
    gidx = lax.broadcasted_iota(jnp.int32, gs.shape, 0)
    grank = jnp.zeros(gs.shape, jnp.int32)
    for j in range(N_EXPERT_GROUPS):
        other = gs[j:j + 1, :]
        ahead = jnp.logical_or(other > gs, jnp.logical_and(other == gs, j < gidx))
        grank = grank + ahead.astype(jnp.int32)
    gsel = (grank < TOPK_GROUPS).reshape(N_EXPERT_GROUPS, 1, tm)
    masked = jnp.where(gsel, g3, -jnp.inf).reshape(e, tm)
    eidx = lax.broadcasted_iota(jnp.int32, masked.shape, 0)
    erank = jnp.zeros(masked.shape, jnp.int32)
    for j in range(e):
        other = masked[j:j + 1, :]
        ahead = jnp.logical_or(other > masked, jnp.logical_and(other == masked, j < eidx))
        erank = erank + ahead.astype(jnp.int32)
    sel = erank < TOP_K
    gate = jnp.where(sel, scores, 0.0)
    denom = jnp.sum(gate, axis=0, keepdims=True)
    return ROUTED_SCALE * gate / denom


def _mixout_kernel(ys_ref, ma_ref, gb_ref, x_ref, wglu_ref, bglu_ref, wout_ref, g_ref, b_ref, rw_ref, rb_ref,
                   x1_ref, comb_ref, ys_scr, *, alpha):
    d = x_ref.shape[1]
    nc, n_sub, width = ys_ref.shape
    ts = width // LANES
    for c in range(nc):
        for s in range(ts):
            ys_scr[c, pl.ds(s, n_sub, stride=ts), :] = ys_ref[c, :, s * LANES:(s + 1) * LANES].astype(F32)
    ys = jnp.concatenate([ys_scr[c] for c in range(nc)], axis=1).astype(BF16)
    glu = jnp.dot(ys, wglu_ref[...], preferred_element_type=F32) + bglu_ref[...]
    yb = glu[:, :d] * _sigmoid(glu[:, d:])
    merged = ma_ref[...].astype(F32) + _sigmoid(gb_ref[...].astype(F32)) * yb
    mix = jnp.dot(merged.astype(BF16), wout_ref[...], preferred_element_type=F32)
    x1 = _layer_norm(alpha * x_ref[...] + mix, g_ref[...], b_ref[...])
    x1_ref[...] = x1
    rw = rw_ref[...]
    rw_hi = rw.astype(BF16)
    rw_lo = (rw - rw_hi.astype(F32)).astype(BF16)
    x_hi = x1.astype(BF16)
    x_lo = (x1 - x_hi.astype(F32)).astype(BF16)
    nt = (((1,), (1,)), ((), ()))
    logits_t = (lax.dot_general(rw_hi, x_hi, nt, preferred_element_type=F32)
                + lax.dot_general(rw_hi, x_lo, nt, preferred_element_type=F32)
                + lax.dot_general(rw_lo, x_hi, nt, preferred_element_type=F32))
    comb_ref[...] = _route(logits_t, rb_ref[...]).T


def _mixout(ys, ma, gb, x, layer, params, alpha, tm):
    t, d = x.shape
    e = params[5].shape[1]
    nc, _, width = ys.shape
    ts = width // LANES
    row = lambda i: (i, 0)
    return pl.pallas_call(
        functools.partial(_mixout_kernel, alpha=alpha),
        grid=(t // tm,),
        in_specs=[pl.BlockSpec((nc, tm // ts, width), lambda i: (0, i, 0))] +
                 [pl.BlockSpec((tm, d), row)] * 3 + [_layer_spec(a, layer) for a in params],
        out_specs=[pl.BlockSpec((tm, d), row), pl.BlockSpec((tm, e), row)],
        out_shape=[jax.ShapeDtypeStruct((t, d), F32), jax.ShapeDtypeStruct((t, e), F32)],
        scratch_shapes=[pltpu.VMEM((nc, tm, LANES), F32)],
        compiler_params=_params("parallel"),
        name="mixout",
    )(ys, ma, gb, x, *params)


def _moe_kernel(x_ref, comb_ref, wg_ref, wu_ref, wd_ref, sg_ref, su_ref, sd_ref, p_ref, pw_ref, pgw_ref, pgb_ref,
                g_ref, b_ref, o_ref, xb_ref, *, alpha):
    c = pl.program_id(1)
    ec = wg_ref.shape[0]

    half = o_ref.shape[1] // 2

    def add_ffn(xb, wg, wu, wd_ref_e, scale):
        gg = jnp.dot(xb, wg, preferred_element_type=F32)
        hdn = gg * _sigmoid(gg) * jnp.dot(xb, wu, preferred_element_type=F32)
        if scale is not None:
            hdn = hdn * scale
        hdn = hdn.astype(BF16)
        for lo in (0, half):
            o_ref[:, lo:lo + half] += jnp.dot(hdn, wd_ref_e[:, lo:lo + half], preferred_element_type=F32)

    @pl.when(c == 0)
    def _():
        x = x_ref[...]
        xb = x.astype(BF16)
        xb_ref[...] = xb
        gate = _sigmoid(jnp.dot(xb, pgw_ref[...], preferred_element_type=F32) + pgb_ref[...])
        ple = gate * jnp.dot(p_ref[...].astype(BF16), pw_ref[...], preferred_element_type=F32)
        o_ref[...] = alpha * x + ple
        add_ffn(xb, sg_ref[...], su_ref[...], sd_ref, None)

    xb = xb_ref[...]
    comb = comb_ref[...]
    lane = lax.broadcasted_iota(jnp.int32, comb.shape, 1)
    for e in range(ec):
        col = jnp.sum(jnp.where(lane == c * ec + e, comb, 0.0), axis=1, keepdims=True)
        add_ffn(xb, wg_ref[e], wu_ref[e], wd_ref.at[e], col)

    @pl.when(c == pl.num_programs(1) - 1)
    def _():
        o_ref[...] = _layer_norm(o_ref[...], g_ref[...], b_ref[...])


def _moe(x1, comb, layer, wg, wu, wd, sg, su, sd, p, pw, pgw, pgb, g, b, alpha, tm, ec):
    t, d = x1.shape
    _, e, _, f = wg.shape
    dp = p.shape[2]
    expert = lambda i, c: (layer, c, 0, 0)
    const = lambda a: _layer_spec(a, layer, single_buffer=True)
    return pl.pallas_call(
        functools.partial(_moe_kernel, alpha=alpha),
        grid=(t // tm, e // ec),
        in_specs=[pl.BlockSpec((tm, d), lambda i, c: (i, 0)),
                  pl.BlockSpec((tm, e), lambda i, c: (i, 0)),
                  pl.BlockSpec((None, ec, d, f), expert),
                  pl.BlockSpec((None, ec, d, f), expert),
                  pl.BlockSpec((None, ec, f, d), expert),
                  const(sg), const(su), const(sd),
                  pl.BlockSpec((None, tm, dp), lambda i, c: (layer, i, 0)),
                  const(pw), const(pgw), const(pgb), const(g), const(b)],
        out_specs=pl.BlockSpec((tm, d), lambda i, c: (i, 0)),
        out_shape=jax.ShapeDtypeStruct((t, d), F32),
        scratch_shapes=[pltpu.VMEM((tm, d), BF16)],
        compiler_params=_params("parallel", "arbitrary"),
        name="moe",
    )(x1, comb, wg, wu, wd, sg, su, sd, p, pw, pgw, pgb, g, b)


def _pick_tile(n, target):
    tm = min(n, target)
    assert n % tm == 0, (n, tm)
    return tm


def kernel(x, p, w_in, conv_w, conv_b, lru_wa, lru_ba, lru_wx, lru_bx, lru_lambda, w_lru_out, s5_a_re, s5_a_im, s5_b_re, s5_b_im, s5_c_re, s5_c_im, s5_d, s5_log_dt, w_glu, b_glu, w_out, ln1_g, ln1_b, router_w, router_bias, moe_w_gate, moe_w_up, moe_w_down, shared_w_gate, shared_w_up, shared_w_down, ple_w, ple_gate_w, ple_gate_b, ln2_g, ln2_b):
    depth = w_in.shape[0]
    bsz, seq, d = x.shape
    t = bsz * seq
    d_rnn = conv_w.shape[-1]
    d_s5 = s5_d.shape[-1]
    alpha = (2.0 * depth) ** 0.25
    widths = (d_rnn, d_rnn, d_s5, d, d)
    assert sum(widths) == w_in.shape[-1]
    assert d_s5 % LANES == 0 and seq % S5_SUB == 0

    heads, dh = lru_wa.shape[1], lru_wa.shape[2]
    hpb = (LANES // math.gcd(dh, LANES))
    assert heads % hpb == 0
    ec = 8
    tm_lru = _pick_tile(seq, 256)
    tm_mix = _pick_tile(t, 512)
    tm_moe = _pick_tile(t, 1024)
    n_pow = max(1, int(math.ceil(math.log2(seq // S5_SUB))))

    bf = lambda a: a.astype(BF16)
    w_in_b, w_lru_out_b, w_glu_b, w_out_b = bf(w_in), bf(w_lru_out), bf(w_glu), bf(w_out)
    wg_b, wu_b, wd_b = bf(moe_w_gate), bf(moe_w_up), bf(moe_w_down)
    sg_b, su_b, sd_b = bf(shared_w_gate), bf(shared_w_up), bf(shared_w_down)
    ple_w_b, ple_gate_w_b = bf(ple_w), bf(ple_gate_w)
    lru_params = (conv_w, _row3(conv_b), _block_diag_heads(lru_wa, hpb), _row3(lru_ba.reshape(depth, -1)),
                  _block_diag_heads(lru_wx, hpb), _row3(lru_bx.reshape(depth, -1)), _row3(lru_lambda), w_lru_out_b)
    mix_params = (w_glu_b, _row3(b_glu), w_out_b, _row3(ln1_g), _row3(ln1_b),
                  jnp.swapaxes(router_w, 1, 2), router_bias[..., None])

    groups, n_state = s5_a_re.shape[1], s5_a_re.shape[2]
    nc = d_s5 // LANES
    fold = lambda a: a.reshape((depth * groups,) + a.shape[2:])
    bd, s_re2, s_im2, cwc, (lre, lim) = _s5_operators(fold(s5_a_re), fold(s5_a_im), fold(s5_b_re), fold(s5_b_im),
                                                     fold(s5_c_re), fold(s5_c_im), fold(s5_log_dt))
    unfold = lambda a: a.reshape((depth, nc) + a.shape[1:])
    d_t = jnp.tile(s5_d.reshape(depth, nc, 1, LANES), (1, 1, 1, S5_SUB))
    s5_ops = (unfold(bd), unfold(s_re2), unfold(s_im2), unfold(cwc), unfold(_s5_wpow(lre, lim, n_pow)), d_t)
    assert groups == nc * S5_LANE_GROUPS and n_state * (LANES // n_state) == LANES

    h = x.reshape(t, d)
    p2 = p.reshape(depth, t, p.shape[-1])
    for i in range(depth):
        u_s5, gate_b, m_a = _inproj_lru(h, i, w_in_b, lru_params, widths, S5_SUB, bsz, tm_lru)
        ys = _s5_branch(u_s5, i, s5_ops, bsz)
        x1, comb = _mixout(ys, m_a, gate_b, h, i, mix_params, alpha, tm_mix)
        h = _moe(x1, comb, i, wg_b, wu_b, wd_b, sg_b, su_b, sd_b, p2, ple_w_b, ple_gate_w_b, _row3(ple_gate_b),
                 _row3(ln2_g), _row3(ln2_b), alpha, tm_moe, ec)
    return h.reshape(bsz, seq, d)
```

```python
import functools
import math

import jax
import jax.numpy as jnp
from jax import lax
from jax.experimental import pallas as pl
from jax.experimental.pallas import tpu as pltpu

F32 = jnp.float32
BF16 = jnp.bfloat16

LRU_C = 8.0
TOP_K = 8
N_EXPERT_GROUPS = 8
TOPK_GROUPS = 4
ROUTED_SCALE = 2.5
LN_EPS = 1e-5

LANES = 128
SUBLANES = 8
MXU_DIM = 256
PROJ_CHUNK = 2 * MXU_DIM
VMEM_LIMIT_BYTES = 60 * 1024 * 1024

S5_SUB = 16
S5_LANE_GROUPS = 8


def _params(*sem):
    return pltpu.CompilerParams(dimension_semantics=sem, vmem_limit_bytes=VMEM_LIMIT_BYTES)


def _layer_spec(stacked, layer, single_buffer=False):
    nd = stacked.ndim - 1
    mode = {"pipeline_mode": pl.Buffered(1)} if single_buffer else {}
    return pl.BlockSpec((None,) + stacked.shape[1:], lambda *_: (layer,) + (0,) * nd, **mode)


def _row3(v):
    return v.reshape(v.shape[0], 1, -1)


def _gelu(x):
    return 0.5 * x * (1.0 + jnp.tanh(math.sqrt(2.0 / math.pi) * (x + 0.044715 * (x * x * x))))


def _sigmoid(x):
    return 1.0 / (1.0 + jnp.exp(-x))


def _layer_norm(v, g, b):
    mu = jnp.mean(v, axis=-1, keepdims=True)
    c = v - mu
    var = jnp.mean(c * c, axis=-1, keepdims=True)
    return c * lax.rsqrt(var + LN_EPS) * g + b


def _shift_rows(v, d, fill):
    n = v.shape[0]
    if d % SUBLANES == 0:
        head = jnp.broadcast_to(jnp.asarray(fill, v.dtype), (d,) + v.shape[1:])
        return jnp.concatenate([head, v[:n - d]], axis=0)
    row = lax.broadcasted_iota(jnp.int32, v.shape, 0)
    return jnp.where(row >= d, pltpu.roll(v, d, 0), fill)


def _scan_rows(a, b, h0, a_scr, b_scr, h_scr):
    n, c = a.shape
    ng = n // SUBLANES
    for ct in range(c // LANES):
        a_scr[ct] = a[:, ct * LANES:(ct + 1) * LANES]
        b_scr[ct] = b[:, ct * LANES:(ct + 1) * LANES]
    for ct in range(c // LANES):
        every = lambda s: pl.ds(s, ng, stride=SUBLANES)
        a_s = [a_scr[ct, every(s), :] for s in range(SUBLANES)]
        acum, hloc = [a_s[0]], [b_scr[ct, every(0), :]]
        for s in range(1, SUBLANES):
            hloc.append(a_s[s] * hloc[-1] + b_scr[ct, every(s), :])
            acum.append(a_s[s] * acum[-1])
        ga, gh = acum[-1], hloc[-1]
        d = 1
        while d < ng:
            gh = gh + ga * _shift_rows(gh, d, 0.0)
            ga = ga * _shift_rows(ga, d, 1.0)
            d *= 2
        h0c = h0[:, ct * LANES:(ct + 1) * LANES]
        after = gh + ga * h0c
        cin = _shift_rows(after, 1, h0c)
        for s in range(SUBLANES):
            h_scr[ct, every(s), :] = hloc[s] + acum[s] * cin
    return jnp.concatenate([h_scr[ct] for ct in range(c // LANES)], axis=1)


def _lru_tile(x_ref, g_ref, ga_ref, first, cw_ref, cb_ref, wa_ref, ba_ref, wx_ref, bx_ref, lam_ref, wo_ref,
              prev_ref, h_ref, a_scr, b_scr, hs_scr, n_blk, blk, side_work):
    sites = 5
    per_phase = -(-len(side_work) // sites)

    def run_side_work():
        for _ in range(min(per_phase, len(side_work))):
            side_work.pop(0)()

    tm = x_ref.shape[0]
    run_side_work()
    x = x_ref[...].astype(F32)
    ext = jnp.concatenate([prev_ref[...], x], axis=0)
    prev_ref[...] = x[tm - SUBLANES:, :]
    kw = cw_ref.shape[0]
    xc = cb_ref[...] + cw_ref[kw - 1:kw, :] * x
    for j in range(1, kw):
        xc = xc + cw_ref[kw - 1 - j:kw - j, :] * pltpu.roll(ext, j, 0)[SUBLANES:, :]

    run_side_work()
    xcb = xc.astype(BF16)
    r_parts, i_parts = [], []
    for k in range(n_blk):
        xk = xcb[:, k * blk:(k + 1) * blk]
        r_parts.append(jnp.dot(xk, wa_ref[k], preferred_element_type=F32))
        i_parts.append(jnp.dot(xk, wx_ref[k], preferred_element_type=F32))
    r = _sigmoid(jnp.concatenate(r_parts, axis=1) + ba_ref[...])
    ig = _sigmoid(jnp.concatenate(i_parts, axis=1) + bx_ref[...])

    run_side_work()
    nl = -lam_ref[...]
    softplus = jnp.maximum(nl, 0.0) + jnp.log1p(jnp.exp(-jnp.abs(nl)))
    log_a = (-LRU_C) * r * softplus
    a = jnp.exp(log_a)
    v = 1.0 - a * a
    mult = jnp.where(v > 0.0, v * lax.rsqrt(v), 0.0)
    row = lax.broadcasted_iota(jnp.int32, a.shape, 0)
    mult = jnp.where(jnp.logical_and(row == 0, first), 1.0, mult)
    b = mult * (ig * xc)

    run_side_work()
    h = _scan_rows(a, b, h_ref[0:1, :], a_scr, b_scr, hs_scr)
    h_ref[...] = jnp.broadcast_to(h[tm - 1:tm, :], h_ref.shape)

    run_side_work()
    y = (_gelu(g_ref[...].astype(F32)) * h).astype(BF16)
    ya = jnp.dot(y, wo_ref[...], preferred_element_type=F32)
    out = _sigmoid(ga_ref[...].astype(F32)) * ya
    while side_work:
        side_work.pop(0)()
    return out


def _inlru_kernel(x_ref, w_ref, cw_ref, cb_ref, wa_ref, ba_ref, wx_ref, bx_ref, lam_ref, wo_ref,
                  u_ref, gb_ref, ma_ref,
                  xl_scr, gl_scr, ga_scr, z_scr, prev_ref, h_ref, a_scr, b_scr, hs_scr, *, bounds, n_blk, blk):
    j = pl.program_id(1)

    @pl.when(j == 0)
    def _():
        xl_scr[...] = jnp.zeros_like(xl_scr)
        gl_scr[...] = jnp.zeros_like(gl_scr)
        ga_scr[...] = jnp.zeros_like(ga_scr)

    @pl.when(j <= 1)
    def _():
        prev_ref[...] = jnp.zeros_like(prev_ref)
        h_ref[...] = jnp.zeros_like(h_ref)

    fill, drain = j % 2, (j + 1) % 2
    xb = x_ref[...].astype(BF16)
    ts = u_ref.shape[2] // LANES

    def project(lo, hi, store):
        def thunk():
            store(jnp.dot(xb, w_ref[:, lo:hi], preferred_element_type=F32))
        return thunk

    def to_stash(scr, off):
        def store(z):
            scr[fill, :, off:off + z.shape[1]] = z.astype(scr.dtype)
        return store

    def to_gate_b(off):
        def store(z):
            gb_ref[:, off:off + z.shape[1]] = z.astype(gb_ref.dtype)
        return store

    def to_s5(off):
        def store(z):
            for k in range(z.shape[1] // LANES):
                c = off // LANES + k
                z_scr[c] = z[:, k * LANES:(k + 1) * LANES]
                for s in range(ts):
                    u_ref[c, :, s * LANES:(s + 1) * LANES] = z_scr[
                        c, pl.ds(s, u_ref.shape[1], stride=ts), :].astype(u_ref.dtype)
        return store

    sinks = (functools.partial(to_stash, xl_scr), functools.partial(to_stash, gl_scr), to_s5,
             functools.partial(to_stash, ga_scr), to_gate_b)
    side_work = []
    for (lo, hi), sink in zip(bounds, sinks):
        for c0 in range(lo, hi, PROJ_CHUNK):
            side_work.append(project(c0, min(c0 + PROJ_CHUNK, hi), sink(c0 - lo)))

    ma = _lru_tile(xl_scr.at[drain], gl_scr.at[drain], ga_scr.at[drain], j == 1,
                   cw_ref, cb_ref, wa_ref, ba_ref, wx_ref, bx_ref, lam_ref, wo_ref,
                   prev_ref, h_ref, a_scr, b_scr, hs_scr, n_blk, blk, side_work)
    ma_ref[...] = ma.astype(ma_ref.dtype)


def _inproj_lru(x, layer, w_all, lru_params, widths, ts, bsz, tm):
    t, d = x.shape
    d_rnn, _, d_s5, d_a, d_b = widths
    seq = t // bsz
    nt = seq // tm
    _, n_blk, blk, _ = lru_params[2].shape
    bounds, lo = [], 0
    for w in widths:
        bounds.append((lo, lo + w))
        lo += w
    cur = lambda b, j: (b * nt + jnp.minimum(j, nt - 1), 0)
    prv = lambda b, j: (b * nt + jnp.maximum(j - 1, 0), 0)
    nc = d_s5 // LANES
    return pl.pallas_call(
        functools.partial(_inlru_kernel, bounds=tuple(bounds), n_blk=n_blk, blk=blk),
        grid=(bsz, nt + 1),
        in_specs=[pl.BlockSpec((tm, d), cur), _layer_spec(w_all, layer, single_buffer=True)]
                 + [_layer_spec(a, layer) for a in lru_params],
        out_specs=[pl.BlockSpec((nc, tm // ts, ts * LANES), lambda b, j: (0,) + cur(b, j)),
                   pl.BlockSpec((tm, d_b), cur), pl.BlockSpec((tm, d_a), prv)],
        out_shape=[jax.ShapeDtypeStruct((nc, t // ts, ts * LANES), BF16),
                   jax.ShapeDtypeStruct((t, d_b), BF16), jax.ShapeDtypeStruct((t, d_a), BF16)],
        scratch_shapes=[pltpu.VMEM((2, tm, d_rnn), BF16), pltpu.VMEM((2, tm, d_rnn), BF16),
                        pltpu.VMEM((2, tm, d_a), BF16),
                        pltpu.VMEM((nc, tm, LANES), F32),
                        pltpu.VMEM((SUBLANES, d_rnn), F32), pltpu.VMEM((SUBLANES, d_rnn), F32)]
                       + [pltpu.VMEM((d_rnn // LANES, tm, LANES), F32)] * 3,
        compiler_params=_params("arbitrary", "arbitrary"),
        name="inproj_lru",
    )(x, w_all, *lru_params)


def _block_diag_heads(w, heads_per_blk):
    l, h, dh, _ = w.shape
    nb = h // heads_per_blk
    eye = jnp.eye(heads_per_blk, dtype=w.dtype)
    wb = w.reshape(l * nb, heads_per_blk, dh, dh)
    out = jnp.einsum("nhij,hk->nhikj", wb, eye)
    return out.reshape(l, nb, heads_per_blk * dh, heads_per_blk * dh).astype(BF16)


def _s5_lag_kernel(cr_ref, ci_ref, br_ref, bi_ref, pr_ref, pi_ref, k_ref):
    gb, p, n = cr_ref.shape
    ts = k_ref.shape[1]
    for g in range(gb):
        cr, ci = cr_ref[g], ci_ref[g]
        lag = lambda ref, j: ref[g, j:j + 1, :]
        are = jnp.concatenate([cr * lag(pr_ref, j) - ci * lag(pi_ref, j) for j in range(ts)], axis=0)
        aim = jnp.concatenate([cr * lag(pi_ref, j) + ci * lag(pr_ref, j) for j in range(ts)], axis=0)
        k = (jnp.dot(are, br_ref[g], preferred_element_type=F32, precision=lax.Precision.HIGHEST)
             - jnp.dot(aim, bi_ref[g], preferred_element_type=F32, precision=lax.Precision.HIGHEST))
        k_ref[g] = k.reshape(ts, p, p)


def _s5_lag_kernels(cr, ci, bbr, bbi, pwr, pwi):
    g, p, n = cr.shape
    ts = pwr.shape[1] - 1
    gb = S5_LANE_GROUPS
    cspec = pl.BlockSpec((gb, p, n), lambda i: (i, 0, 0))
    bspec = pl.BlockSpec((gb, n, p), lambda i: (i, 0, 0))
    pspec = pl.BlockSpec((gb, ts + 1, n), lambda i: (i, 0, 0))
    return pl.pallas_call(
        _s5_lag_kernel,
        grid=(g // gb,),
        in_specs=[cspec, cspec, bspec, bspec, pspec, pspec],
        out_specs=pl.BlockSpec((gb, ts, p, p), lambda i: (i, 0, 0, 0)),
        out_shape=jax.ShapeDtypeStruct((g, ts, p, p), F32),
        compiler_params=_params("parallel"),
        name="s5_lag_kernels",
    )(cr, ci, bbr, bbi, pwr, pwi)


def _s5_operators(a_re, a_im, b_re, b_im, c_re, c_im, log_dt):
    ts, gl = S5_SUB, S5_LANE_GROUPS
    g, n = a_re.shape
    p = b_re.shape[-1]
    nc = g // gl
    dt = jnp.exp(log_dt)[:, None]
    lre, lim = dt * a_re, dt * a_im
    mag = jnp.exp(lre)
    abar_re, abar_im = mag * jnp.cos(lim), mag * jnp.sin(lim)
    den = a_re * a_re + a_im * a_im
    z_re = ((abar_re - 1.0) * a_re + abar_im * a_im) / den
    z_im = (abar_im * a_re - (abar_re - 1.0) * a_im) / den
    bb_re = z_re[..., None] * b_re - z_im[..., None] * b_im
    bb_im = z_re[..., None] * b_im + z_im[..., None] * b_re
    steps = jnp.arange(ts + 1, dtype=F32)[None, :, None]
    pmag = jnp.exp(steps * lre[:, None, :])
    pw_re = pmag * jnp.cos(steps * lim[:, None, :])
    pw_im = pmag * jnp.sin(steps * lim[:, None, :])

    k = _s5_lag_kernels(c_re, c_im, bb_re, bb_im, pw_re, pw_im)
    bd = jnp.swapaxes(k, 2, 3).reshape(nc, gl, ts, p, p).transpose(0, 2, 1, 3, 4).reshape(nc, ts, gl * p, p)

    rev_re, rev_im = pw_re[:, ts - 1::-1][:, :ts], pw_im[:, ts - 1::-1][:, :ts]
    bt_re, bt_im = jnp.swapaxes(bb_re, 1, 2), jnp.swapaxes(bb_im, 1, 2)
    sb_re = rev_re[:, :, None, :] * bt_re[:, None] - rev_im[:, :, None, :] * bt_im[:, None]
    sb_im = rev_re[:, :, None, :] * bt_im[:, None] + rev_im[:, :, None, :] * bt_re[:, None]

    def rows_sgq(a):
        a = a.reshape(nc, gl, ts, p, n).transpose(0, 2, 1, 3, 4).reshape(nc, ts * gl * p, n)
        return jnp.concatenate([a] * (LANES // n), axis=-1)
    s_re2, s_im2 = rows_sgq(sb_re), rows_sgq(sb_im)

    c4 = lambda a: a.reshape(nc, gl, p, n).transpose(0, 3, 1, 2)[:, :, None]
    p4 = lambda a: a[:, 1:].reshape(nc, gl, ts, n).transpose(0, 3, 2, 1)[..., None]
    ct_re = c4(c_re) * p4(pw_re) - c4(c_im) * p4(pw_im)
    ct_im = c4(c_re) * p4(pw_im) + c4(c_im) * p4(pw_re)
    cwc = jnp.stack([ct_re, -ct_im], axis=1).reshape(nc, 2, n, ts * gl * p)
    return bd, s_re2, s_im2, cwc, (lre, lim)


def _s5_wpow(lre, lim, n_pow):
    ts, gl = S5_SUB, S5_LANE_GROUPS
    g, n = lre.shape
    nc = g // gl
    e = (ts * (2.0 ** jnp.arange(n_pow, dtype=F32)))[None, :, None]
    mag = jnp.exp(e * lre[:, None, :])
    wr = (mag * jnp.cos(e * lim[:, None, :])).reshape(nc, gl, n_pow, n)
    wi = (mag * jnp.sin(e * lim[:, None, :])).reshape(nc, gl, n_pow, n)
    w = jnp.stack([wr, wi], axis=0)
    return jnp.transpose(w, (1, 3, 0, 2, 4)).reshape(nc, n_pow, 2 * gl * n)


def _s5_assemble(bd_ref, sre_ref, sim_ref, cw_ref, m_scr, s_scr, c_scr):
    ts = bd_ref.shape[1]
    n = cw_ref.shape[2]
    half = s_scr.shape[1] // 2
    gl = half // n
    p = LANES // gl
    square = (LANES, LANES)
    spread = (lax.broadcasted_iota(jnp.int32, (p, LANES), 1) % p == lax.broadcasted_iota(jnp.int32, (p, LANES), 0))
    spread = jnp.where(spread, 1.0, 0.0).astype(m_scr.dtype)
    same_group = lax.broadcasted_iota(jnp.int32, square, 0) // p == lax.broadcasted_iota(jnp.int32, square, 1) // p
    lag_blocks = [jnp.where(same_group, jnp.dot(bd_ref[0, j].astype(m_scr.dtype), spread, preferred_element_type=F32),
                            0.0).astype(m_scr.dtype) for j in range(ts)]
    for s in range(ts):
        for t in range(ts):
            blk = (slice(s * LANES, (s + 1) * LANES), slice(t * LANES, (t + 1) * LANES))
            if t >= s:
                m_scr[blk] = lag_blocks[t - s]
            elif t * LANES // MXU_DIM == s * LANES // MXU_DIM:
                m_scr[blk] = jnp.zeros((LANES, LANES), m_scr.dtype)
    shape = (s_scr.shape[0], half)
    own = ((lax.broadcasted_iota(jnp.int32, shape, 0) // p) % gl) == (lax.broadcasted_iota(jnp.int32, shape, 1) // n)
    for ref, lo in ((sre_ref, 0), (sim_ref, half)):
        wide = jnp.concatenate([ref[0]] * (half // LANES), axis=1)
        s_scr[:, lo:lo + half] = jnp.where(own, wide, 0.0).astype(s_scr.dtype)
    colg = (lax.broadcasted_iota(jnp.int32, (n, c_scr.shape[1]), 1) // p) % gl
    for r in range(2):
        for g in range(gl):
            c_scr[r * half + g * n:r * half + (g + 1) * n, :] = jnp.where(colg == g, cw_ref[0, r], 0.0).astype(c_scr.dtype)


def _s5_state_scan(ds, w_ref, ds_scr, hs_scr, n_pow):
    rows, width = ds.shape
    half = width // 2
    nct = half // LANES
    ng = rows // SUBLANES
    assert rows <= 2 ** n_pow and rows % SUBLANES == 0
    for ct in range(2 * nct):
        ds_scr[ct] = ds[:, ct * LANES:(ct + 1) * LANES]
    every = lambda s: pl.ds(s, ng, stride=SUBLANES)
    cmul = lambda ar, ai, br, bi: (ar * br - ai * bi, ar * bi + ai * br)
    for ct in range(nct):
        cols = slice(ct * LANES, (ct + 1) * LANES)
        icols = slice(half + ct * LANES, half + (ct + 1) * LANES)
        w1r, w1i = w_ref[0, 0:1, cols], w_ref[0, 0:1, icols]
        hloc = [(ds_scr[ct, every(0), :], ds_scr[nct + ct, every(0), :])]
        for s in range(1, SUBLANES):
            pr, pi = cmul(w1r, w1i, *hloc[-1])
            hloc.append((pr + ds_scr[ct, every(s), :], pi + ds_scr[nct + ct, every(s), :]))
        gr, gi = hloc[-1]
        d, k = 1, 3
        while d < ng:
            sr, si = cmul(w_ref[0, k:k + 1, cols], w_ref[0, k:k + 1, icols],
                          _shift_rows(gr, d, 0.0), _shift_rows(gi, d, 0.0))
            gr, gi = gr + sr, gi + si
            d, k = 2 * d, k + 1
        cr, ci = _shift_rows(gr, 1, 0.0), _shift_rows(gi, 1, 0.0)
        hs_scr[ct, every(0), :] = cr
        hs_scr[nct + ct, every(0), :] = ci
        pr, pi = w1r, w1i
        for s in range(1, SUBLANES):
            ar, ai = cmul(pr, pi, cr, ci)
            hs_scr[ct, every(s), :] = hloc[s - 1][0] + ar
            hs_scr[nct + ct, every(s), :] = hloc[s - 1][1] + ai
            if s + 1 < SUBLANES:
                pr, pi = cmul(pr, pi, w1r, w1i)
    return jnp.concatenate([hs_scr[ct] for ct in range(2 * nct)], axis=1)


def _s5_kernel(x_ref, bd_ref, sre_ref, sim_ref, cw_ref, w_ref, d_ref, y_ref, m_scr, s_scr, c_scr, ds_scr, hs_scr,
               *, n_pow):
    @pl.when(pl.program_id(1) == 0)
    def _():
        _s5_assemble(bd_ref, sre_ref, sim_ref, cw_ref, m_scr, s_scr, c_scr)

    x = x_ref[0]
    ds = jnp.dot(x, s_scr[...], preferred_element_type=F32)
    hs = _s5_state_scan(ds, w_ref, ds_scr, hs_scr, n_pow).astype(BF16)
    for j in range(x.shape[1] // MXU_DIM):
        lo, hi_ = j * MXU_DIM, (j + 1) * MXU_DIM
        y = (jnp.dot(x[:, :hi_], m_scr[:hi_, lo:hi_], preferred_element_type=F32)
             + jnp.dot(hs, c_scr[:, lo:hi_], preferred_element_type=F32))
        y = y + d_ref[0, :, lo:hi_] * x[:, lo:hi_].astype(F32)
        y_ref[0, :, lo:hi_] = _gelu(y).astype(y_ref.dtype)


def _s5_branch(u2, layer, ops, bsz):
    nc, n_sub, width = u2.shape
    rows = n_sub // bsz
    cwc, wpow = ops[3], ops[4]
    n_state2 = 2 * S5_LANE_GROUPS * cwc.shape[3]
    x_spec = pl.BlockSpec((1, rows, width), lambda c, b: (c, b, 0))
    op_spec = lambda a: pl.BlockSpec((None, 1) + a.shape[2:], lambda c, b: (layer, c) + (0,) * (a.ndim - 2))
    return pl.pallas_call(
        functools.partial(_s5_kernel, n_pow=wpow.shape[2]),
        grid=(nc, bsz),
        in_specs=[x_spec] + [op_spec(a) for a in ops],
        out_specs=x_spec,
        out_shape=jax.ShapeDtypeStruct(u2.shape, BF16),
        scratch_shapes=[pltpu.VMEM((width, width), BF16), pltpu.VMEM((width, n_state2), BF16),
                        pltpu.VMEM((n_state2, width), BF16)]
                       + [pltpu.VMEM((n_state2 // LANES, rows, LANES), F32)] * 2,
        compiler_params=_params("arbitrary", "arbitrary"),
        name="s5_branch",
    )(u2, *ops)


def _route(logits_t, bias):
    e, tm = logits_t.shape
    per = e // N_EXPERT_GROUPS
    scores = _sigmoid(logits_t)
    biased = scores + bias
    g3 = biased.reshape(N_EXPERT_GROUPS, per, tm)
    sub = lax.broadcasted_iota(jnp.int32, g3.shape, 1)
    m1 = jnp.max(g3, axis=1, keepdims=True)
    first = jnp.min(jnp.where(g3 == m1, sub, per), axis=1, keepdims=True)
    m2 = jnp.max(jnp.where(sub == first, -jnp.inf, g3), axis=1, keepdims=True)
    gs = (m1 + m2).reshape(N_EXPERT_GROUPS, tm)
    gidx = lax.broadcasted_iota(jnp.int32, gs.shape, 0)
    grank = jnp.zeros(gs.shape, jnp.int32)
    for j in range(N_EXPERT_GROUPS):
        other = gs[j:j + 1, :]
        ahead = jnp.logical_or(other > gs, jnp.logical_and(other == gs, j < gidx))
        grank = grank + ahead.astype(jnp.int32)
    gsel = (grank < TOPK_GROUPS).reshape(N_EXPERT_GROUPS, 1, tm)
    masked = jnp.where(gsel, g3, -jnp.inf).reshape(e, tm)
    eidx = lax.broadcasted_iota(jnp.int32, masked.shape, 0)
    work, sel = masked, None
    for _ in range(TOP_K):
        top = jnp.max(work, axis=0, keepdims=True)
        cand = work == top if sel is None else jnp.logical_and(work == top, jnp.logical_not(sel))
        first = jnp.min(jnp.where(cand, eidx, e), axis=0, keepdims=True)
        pick = eidx == first
        sel = pick if sel is None else jnp.logical_or(sel, pick)
        work = jnp.where(pick, -jnp.inf, work)
    gate = jnp.where(sel, scores, 0.0)
    denom = jnp.sum(gate, axis=0, keepdims=True)
    return ROUTED_SCALE * gate / denom


def _mixout_kernel(ys_ref, ma_ref, gb_ref, x_ref, wglu_ref, bglu_ref, wout_ref, g_ref, b_ref, rw_ref, rb_ref,
                   x1_ref, comb_ref, ys_scr, *, alpha):
    d = x_ref.shape[1]
    nc, n_sub, width = ys_ref.shape
    ts = width // LANES
    for c in range(nc):
        for s in range(ts):
            ys_scr[c, pl.ds(s, n_sub, stride=ts), :] = ys_ref[c, :, s * LANES:(s + 1) * LANES].astype(F32)
    ys = jnp.concatenate([ys_scr[c] for c in range(nc)], axis=1).astype(BF16)
    glu = jnp.dot(ys, wglu_ref[...], preferred_element_type=F32) + bglu_ref[...]
    yb = glu[:, :d] * _sigmoid(glu[:, d:])
    merged = ma_ref[...].astype(F32) + _sigmoid(gb_ref[...].astype(F32)) * yb
    mix = jnp.dot(merged.astype(BF16), wout_ref[...], preferred_element_type=F32)
    x1 = _layer_norm(alpha * x_ref[...] + mix, g_ref[...], b_ref[...])
    x1_ref[...] = x1
    rw = rw_ref[...]
    rw_hi = rw.astype(BF16)
    rw_lo = (rw - rw_hi.astype(F32)).astype(BF16)
    x_hi = x1.astype(BF16)
    x_lo = (x1 - x_hi.astype(F32)).astype(BF16)
    nt = (((1,), (1,)), ((), ()))
    logits_t = (lax.dot_general(rw_hi, x_hi, nt, preferred_element_type=F32)
                + lax.dot_general(rw_hi, x_lo, nt, preferred_element_type=F32)
                + lax.dot_general(rw_lo, x_hi, nt, preferred_element_type=F32))
    comb_ref[...] = _route(logits_t, rb_ref[...]).T


def _mixout(ys, ma, gb, x, layer, params, alpha, tm):
    t, d = x.shape
    e = params[5].shape[1]
    nc, _, width = ys.shape
    ts = width // LANES
    row = lambda i: (i, 0)
    return pl.pallas_call(
        functools.partial(_mixout_kernel, alpha=alpha),
        grid=(t // tm,),
        in_specs=[pl.BlockSpec((nc, tm // ts, width), lambda i: (0, i, 0))] +
                 [pl.BlockSpec((tm, d), row)] * 3 + [_layer_spec(a, layer) for a in params],
        out_specs=[pl.BlockSpec((tm, d), row), pl.BlockSpec((tm, e), row)],
        out_shape=[jax.ShapeDtypeStruct((t, d), F32), jax.ShapeDtypeStruct((t, e), F32)],
        scratch_shapes=[pltpu.VMEM((nc, tm, LANES), F32)],
        compiler_params=_params("parallel"),
        name="mixout",
    )(ys, ma, gb, x, *params)


def _moe_kernel(x_ref, comb_ref, wg_ref, wu_ref, wd_ref, sg_ref, su_ref, sd_ref, p_ref, pw_ref, pgw_ref, pgb_ref,
                g_ref, b_ref, o_ref, xb_ref, *, alpha):
    c = pl.program_id(1)
    ec = wg_ref.shape[0]

    half = o_ref.shape[1] // 2

    def add_ffn(xb, wg, wu, wd_ref_e, scale):
        gg = jnp.dot(xb, wg, preferred_element_type=F32)
        hdn = gg * _sigmoid(gg) * jnp.dot(xb, wu, preferred_element_type=F32)
        if scale is not None:
            hdn = hdn * scale
        hdn = hdn.astype(BF16)
        for lo in (0, half):
            o_ref[:, lo:lo + half] += jnp.dot(hdn, wd_ref_e[:, lo:lo + half], preferred_element_type=F32)

    @pl.when(c == 0)
    def _():
        x = x_ref[...]
        xb = x.astype(BF16)
        xb_ref[...] = xb
        gate = _sigmoid(jnp.dot(xb, pgw_ref[...], preferred_element_type=F32) + pgb_ref[...])
        ple = gate * jnp.dot(p_ref[...].astype(BF16), pw_ref[...], preferred_element_type=F32)
        o_ref[...] = alpha * x + ple
        add_ffn(xb, sg_ref[...], su_ref[...], sd_ref, None)

    xb = xb_ref[...]
    comb = comb_ref[...]
    lane = lax.broadcasted_iota(jnp.int32, comb.shape, 1)
    for e in range(ec):
        col = jnp.sum(jnp.where(lane == c * ec + e, comb, 0.0), axis=1, keepdims=True)
        add_ffn(xb, wg_ref[e], wu_ref[e], wd_ref.at[e], col)

    @pl.when(c == pl.num_programs(1) - 1)
    def _():
        o_ref[...] = _layer_norm(o_ref[...], g_ref[...], b_ref[...])


def _moe(x1, comb, layer, wg, wu, wd, sg, su, sd, p, pw, pgw, pgb, g, b, alpha, tm, ec):
    t, d = x1.shape
    _, e, _, f = wg.shape
    dp = p.shape[2]
    expert = lambda i, c: (layer, c, 0, 0)
    const = lambda a: _layer_spec(a, layer, single_buffer=True)
    return pl.pallas_call(
        functools.partial(_moe_kernel, alpha=alpha),
        grid=(t // tm, e // ec),
        in_specs=[pl.BlockSpec((tm, d), lambda i, c: (i, 0)),
                  pl.BlockSpec((tm, e), lambda i, c: (i, 0)),
                  pl.BlockSpec((None, ec, d, f), expert),
                  pl.BlockSpec((None, ec, d, f), expert),
                  pl.BlockSpec((None, ec, f, d), expert),
                  const(sg), const(su), const(sd),
                  pl.BlockSpec((None, tm, dp), lambda i, c: (layer, i, 0)),
                  const(pw), const(pgw), const(pgb), const(g), const(b)],
        out_specs=pl.BlockSpec((tm, d), lambda i, c: (i, 0)),
        out_shape=jax.ShapeDtypeStruct((t, d), F32),
        scratch_shapes=[pltpu.VMEM((tm, d), BF16)],
        compiler_params=_params("parallel", "arbitrary"),
        name="moe",
    )(x1, comb, wg, wu, wd, sg, su, sd, p, pw, pgw, pgb, g, b)


def _pick_tile(n, target):
    tm = min(n, target)
    assert n % tm == 0, (n, tm)
    return tm


def kernel(x, p, w_in, conv_w, conv_b, lru_wa, lru_ba, lru_wx, lru_bx, lru_lambda, w_lru_out, s5_a_re, s5_a_im, s5_b_re, s5_b_im, s5_c_re, s5_c_im, s5_d, s5_log_dt, w_glu, b_glu, w_out, ln1_g, ln1_b, router_w, router_bias, moe_w_gate, moe_w_up, moe_w_down, shared_w_gate, shared_w_up, shared_w_down, ple_w, ple_gate_w, ple_gate_b, ln2_g, ln2_b):
    depth = w_in.shape[0]
    bsz, seq, d = x.shape
    t = bsz * seq
    d_rnn = conv_w.shape[-1]
    d_s5 = s5_d.shape[-1]
    alpha = (2.0 * depth) ** 0.25
    widths = (d_rnn, d_rnn, d_s5, d, d)
    assert sum(widths) == w_in.shape[-1]
    assert d_s5 % LANES == 0 and seq % S5_SUB == 0

    heads, dh = lru_wa.shape[1], lru_wa.shape[2]
    hpb = (LANES // math.gcd(dh, LANES))
    assert heads % hpb == 0
    ec = 8
    tm_lru = _pick_tile(seq, 256)
    tm_mix = _pick_tile(t, 512)
    tm_moe = _pick_tile(t, 1024)
    n_pow = max(1, int(math.ceil(math.log2(seq // S5_SUB))))

    bf = lambda a: a.astype(BF16)
    w_in_b, w_lru_out_b, w_glu_b, w_out_b = bf(w_in), bf(w_lru_out), bf(w_glu), bf(w_out)
    wg_b, wu_b, wd_b = bf(moe_w_gate), bf(moe_w_up), bf(moe_w_down)
    sg_b, su_b, sd_b = bf(shared_w_gate), bf(shared_w_up), bf(shared_w_down)
    ple_w_b, ple_gate_w_b = bf(ple_w), bf(ple_gate_w)
    lru_params = (conv_w, _row3(conv_b), _block_diag_heads(lru_wa, hpb), _row3(lru_ba.reshape(depth, -1)),
                  _block_diag_heads(lru_wx, hpb), _row3(lru_bx.reshape(depth, -1)), _row3(lru_lambda), w_lru_out_b)
    mix_params = (w_glu_b, _row3(b_glu), w_out_b, _row3(ln1_g), _row3(ln1_b),
                  jnp.swapaxes(router_w, 1, 2), router_bias[..., None])

    groups, n_state = s5_a_re.shape[1], s5_a_re.shape[2]
    nc = d_s5 // LANES
    fold = lambda a: a.reshape((depth * groups,) + a.shape[2:])
    bd, s_re2, s_im2, cwc, (lre, lim) = _s5_operators(fold(s5_a_re), fold(s5_a_im), fold(s5_b_re), fold(s5_b_im),
                                                     fold(s5_c_re), fold(s5_c_im), fold(s5_log_dt))
    unfold = lambda a: a.reshape((depth, nc) + a.shape[1:])
    d_t = jnp.tile(s5_d.reshape(depth, nc, 1, LANES), (1, 1, 1, S5_SUB))
    s5_ops = (unfold(bd), unfold(s_re2), unfold(s_im2), unfold(cwc), unfold(_s5_wpow(lre, lim, n_pow)), d_t)
    assert groups == nc * S5_LANE_GROUPS and n_state * (LANES // n_state) == LANES

    h = x.reshape(t, d)
    p2 = p.reshape(depth, t, p.shape[-1])
    for i in range(depth):
        u_s5, gate_b, m_a = _inproj_lru(h, i, w_in_b, lru_params, widths, S5_SUB, bsz, tm_lru)
        ys = _s5_branch(u_s5, i, s5_ops, bsz)
        x1, comb = _mixout(ys, m_a, gate_b, h, i, mix_params, alpha, tm_mix)
        h = _moe(x1, comb, i, wg_b, wu_b, wd_b, sg_b, su_b, sd_b, p2, ple_w_b, ple_gate_w_b, _row3(ple_gate_b),
                 _row3(ln2_g), _row3(ln2_b), alpha, tm_moe, ec)
    return h.reshape(bsz, seq, d)
```

```python
import functools
import math

import jax
import jax.numpy as jnp
from jax import lax
from jax.experimental import pallas as pl
from jax.experimental.pallas import tpu as pltpu

F32 = jnp.float32
BF16 = jnp.bfloat16

LRU_C = 8.0
TOP_K = 8
N_EXPERT_GROUPS = 8
TOPK_GROUPS = 4
ROUTED_SCALE = 2.5
LN_EPS = 1e-5

LANES = 128
SUBLANES = 8
MXU_DIM = 256
PROJ_CHUNK = 2 * MXU_DIM
VMEM_LIMIT_BYTES = 60 * 1024 * 1024

S5_SUB = 16
S5_LANE_GROUPS = 8

TM_SEQ = 256
TM_MIX = 512
TM_MOE = 1024
MOE_EXPERTS_PER_STEP = 8


def _params(*sem):
    return pltpu.CompilerParams(dimension_semantics=sem, vmem_limit_bytes=VMEM_LIMIT_BYTES)


def _layer_spec(stacked, layer, single_buffer=False):
    nd = stacked.ndim - 1
    mode = {"pipeline_mode": pl.Buffered(1)} if single_buffer else {}
    return pl.BlockSpec((None,) + stacked.shape[1:], lambda *_: (layer,) + (0,) * nd, **mode)


def _row3(v):
    return v.reshape(v.shape[0], 1, -1)


def _gelu(x):
    return 0.5 * x * (1.0 + jnp.tanh(math.sqrt(2.0 / math.pi) * (x + 0.044715 * (x * x * x))))


def _sigmoid(x):
    return 1.0 / (1.0 + jnp.exp(-x))


def _layer_norm(v, g, b):
    mu = jnp.mean(v, axis=-1, keepdims=True)
    c = v - mu
    var = jnp.mean(c * c, axis=-1, keepdims=True)
    return c * lax.rsqrt(var + LN_EPS) * g + b


def _shift_rows(v, d, fill):
    n = v.shape[0]
    if d % SUBLANES == 0:
        head = jnp.broadcast_to(jnp.asarray(fill, v.dtype), (d,) + v.shape[1:])
        return jnp.concatenate([head, v[:n - d]], axis=0)
    row = lax.broadcasted_iota(jnp.int32, v.shape, 0)
    return jnp.where(row >= d, pltpu.roll(v, d, 0), fill)


def _scan_rows(a, b, h0, a_scr, b_scr, h_scr):
    n, c = a.shape
    ng = n // SUBLANES
    for ct in range(c // LANES):
        a_scr[ct] = a[:, ct * LANES:(ct + 1) * LANES]
        b_scr[ct] = b[:, ct * LANES:(ct + 1) * LANES]
    for ct in range(c // LANES):
        every = lambda s: pl.ds(s, ng, stride=SUBLANES)
        a_s = [a_scr[ct, every(s), :] for s in range(SUBLANES)]
        acum, hloc = [a_s[0]], [b_scr[ct, every(0), :]]
        for s in range(1, SUBLANES):
            hloc.append(a_s[s] * hloc[-1] + b_scr[ct, every(s), :])
            acum.append(a_s[s] * acum[-1])
        ga, gh = acum[-1], hloc[-1]
        d = 1
        while d < ng:
            gh = gh + ga * _shift_rows(gh, d, 0.0)
            ga = ga * _shift_rows(ga, d, 1.0)
            d *= 2
        h0c = h0[:, ct * LANES:(ct + 1) * LANES]
        after = gh + ga * h0c
        cin = _shift_rows(after, 1, h0c)
        for s in range(SUBLANES):
            h_scr[ct, every(s), :] = hloc[s] + acum[s] * cin
    return jnp.concatenate([h_scr[ct] for ct in range(c // LANES)], axis=1)


def _lru_tile(x_ref, g_ref, ga_ref, first, cw_ref, cb_ref, wa_ref, ba_ref, wx_ref, bx_ref, lam_ref, wo_ref,
              prev_ref, h_ref, a_scr, b_scr, hs_scr, n_blk, blk, side_work):
    sites = 5
    per_phase = -(-len(side_work) // sites)

    def run_side_work():
        for _ in range(min(per_phase, len(side_work))):
            side_work.pop(0)()

    tm = x_ref.shape[0]
    run_side_work()
    x = x_ref[...].astype(F32)
    ext = jnp.concatenate([prev_ref[...], x], axis=0)
    prev_ref[...] = x[tm - SUBLANES:, :]
    kw = cw_ref.shape[0]
    xc = cb_ref[...] + cw_ref[kw - 1:kw, :] * x
    for j in range(1, kw):
        xc = xc + cw_ref[kw - 1 - j:kw - j, :] * pltpu.roll(ext, j, 0)[SUBLANES:, :]

    run_side_work()
    xcb = xc.astype(BF16)
    r_parts, i_parts = [], []
    for k in range(n_blk):
        xk = xcb[:, k * blk:(k + 1) * blk]
        r_parts.append(jnp.dot(xk, wa_ref[k], preferred_element_type=F32))
        i_parts.append(jnp.dot(xk, wx_ref[k], preferred_element_type=F32))
    r = _sigmoid(jnp.concatenate(r_parts, axis=1) + ba_ref[...])
    ig = _sigmoid(jnp.concatenate(i_parts, axis=1) + bx_ref[...])

    run_side_work()
    nl = -lam_ref[...]
    softplus = jnp.maximum(nl, 0.0) + jnp.log1p(jnp.exp(-jnp.abs(nl)))
    log_a = (-LRU_C) * r * softplus
    a = jnp.exp(log_a)
    v = 1.0 - a * a
    mult = jnp.where(v > 0.0, v * lax.rsqrt(v), 0.0)
    row = lax.broadcasted_iota(jnp.int32, a.shape, 0)
    mult = jnp.where(jnp.logical_and(row == 0, first), 1.0, mult)
    b = mult * (ig * xc)

    run_side_work()
    h = _scan_rows(a, b, h_ref[0:1, :], a_scr, b_scr, hs_scr)
    h_ref[...] = jnp.broadcast_to(h[tm - 1:tm, :], h_ref.shape)

    run_side_work()
    y = (_gelu(g_ref[...].astype(F32)) * h).astype(BF16)
    ya = jnp.dot(y, wo_ref[...], preferred_element_type=F32)
    out = _sigmoid(ga_ref[...].astype(F32)) * ya
    while side_work:
        side_work.pop(0)()
    return out


def _inlru_kernel(x_ref, w_ref, cw_ref, cb_ref, wa_ref, ba_ref, wx_ref, bx_ref, lam_ref, wo_ref,
                  u_ref, gb_ref, ma_ref,
                  xl_scr, gl_scr, ga_scr, z_scr, prev_ref, h_ref, a_scr, b_scr, hs_scr, *, bounds, n_blk, blk):
    j = pl.program_id(1)

    @pl.when(j == 0)
    def _():
        xl_scr[...] = jnp.zeros_like(xl_scr)
        gl_scr[...] = jnp.zeros_like(gl_scr)
        ga_scr[...] = jnp.zeros_like(ga_scr)

    @pl.when(j <= 1)
    def _():
        prev_ref[...] = jnp.zeros_like(prev_ref)
        h_ref[...] = jnp.zeros_like(h_ref)

    fill, drain = j % 2, (j + 1) % 2
    xb = x_ref[...].astype(BF16)
    ts = u_ref.shape[2] // LANES

    def project(lo, hi, store):
        def thunk():
            store(jnp.dot(xb, w_ref[:, lo:hi], preferred_element_type=F32))
        return thunk

    def to_stash(scr, off):
        def store(z):
            scr[fill, :, off:off + z.shape[1]] = z.astype(scr.dtype)
        return store

    def to_gate_b(off):
        def store(z):
            gb_ref[:, off:off + z.shape[1]] = z.astype(gb_ref.dtype)
        return store

    def to_s5(off):
        def store(z):
            for k in range(z.shape[1] // LANES):
                c = off // LANES + k
                z_scr[c] = z[:, k * LANES:(k + 1) * LANES]
                for s in range(ts):
                    u_ref[c, :, s * LANES:(s + 1) * LANES] = z_scr[
                        c, pl.ds(s, u_ref.shape[1], stride=ts), :].astype(u_ref.dtype)
        return store

    sinks = (functools.partial(to_stash, xl_scr), functools.partial(to_stash, gl_scr), to_s5,
             functools.partial(to_stash, ga_scr), to_gate_b)
    side_work = []
    for (lo, hi), sink in zip(bounds, sinks):
        for c0 in range(lo, hi, PROJ_CHUNK):
            side_work.append(project(c0, min(c0 + PROJ_CHUNK, hi), sink(c0 - lo)))

    ma = _lru_tile(xl_scr.at[drain], gl_scr.at[drain], ga_scr.at[drain], j == 1,
                   cw_ref, cb_ref, wa_ref, ba_ref, wx_ref, bx_ref, lam_ref, wo_ref,
                   prev_ref, h_ref, a_scr, b_scr, hs_scr, n_blk, blk, side_work)
    ma_ref[...] = ma.astype(ma_ref.dtype)


def _inproj_lru(x, layer, w_all, lru_params, widths, ts, bsz, tm):
    t, d = x.shape
    d_rnn, _, d_s5, d_a, d_b = widths
    seq = t // bsz
    nt = seq // tm
    _, n_blk, blk, _ = lru_params[2].shape
    bounds, lo = [], 0
    for w in widths:
        bounds.append((lo, lo + w))
        lo += w
    cur = lambda b, j: (b * nt + jnp.minimum(j, nt - 1), 0)
    prv = lambda b, j: (b * nt + jnp.maximum(j - 1, 0), 0)
    nc = d_s5 // LANES
    return pl.pallas_call(
        functools.partial(_inlru_kernel, bounds=tuple(bounds), n_blk=n_blk, blk=blk),
        grid=(bsz, nt + 1),
        in_specs=[pl.BlockSpec((tm, d), cur), _layer_spec(w_all, layer, single_buffer=True)]
                 + [_layer_spec(a, layer) for a in lru_params],
        out_specs=[pl.BlockSpec((nc, tm // ts, ts * LANES), lambda b, j: (0,) + cur(b, j)),
                   pl.BlockSpec((tm, d_b), cur), pl.BlockSpec((tm, d_a), prv)],
        out_shape=[jax.ShapeDtypeStruct((nc, t // ts, ts * LANES), BF16),
                   jax.ShapeDtypeStruct((t, d_b), BF16), jax.ShapeDtypeStruct((t, d_a), BF16)],
        scratch_shapes=[pltpu.VMEM((2, tm, d_rnn), BF16), pltpu.VMEM((2, tm, d_rnn), BF16),
                        pltpu.VMEM((2, tm, d_a), BF16),
                        pltpu.VMEM((nc, tm, LANES), F32),
                        pltpu.VMEM((SUBLANES, d_rnn), F32), pltpu.VMEM((SUBLANES, d_rnn), F32)]
                       + [pltpu.VMEM((d_rnn // LANES, tm, LANES), F32)] * 3,
        compiler_params=_params("arbitrary", "arbitrary"),
        name="inproj_lru",
    )(x, w_all, *lru_params)


def _block_diag_heads(w, heads_per_blk):
    l, h, dh, _ = w.shape
    nb = h // heads_per_blk
    eye = jnp.eye(heads_per_blk, dtype=w.dtype)
    wb = w.reshape(l * nb, heads_per_blk, dh, dh)
    out = jnp.einsum("nhij,hk->nhikj", wb, eye)
    return out.reshape(l, nb, heads_per_blk * dh, heads_per_blk * dh).astype(BF16)


def _s5_lag_kernel(cr_ref, ci_ref, br_ref, bi_ref, pr_ref, pi_ref, k_ref):
    gb, p, n = cr_ref.shape
    ts = k_ref.shape[1]
    for g in range(gb):
        cr, ci = cr_ref[g], ci_ref[g]
        lag = lambda ref, j: ref[g, j:j + 1, :]
        are = jnp.concatenate([cr * lag(pr_ref, j) - ci * lag(pi_ref, j) for j in range(ts)], axis=0)
        aim = jnp.concatenate([cr * lag(pi_ref, j) + ci * lag(pr_ref, j) for j in range(ts)], axis=0)
        k = (jnp.dot(are, br_ref[g], preferred_element_type=F32, precision=lax.Precision.HIGHEST)
             - jnp.dot(aim, bi_ref[g], preferred_element_type=F32, precision=lax.Precision.HIGHEST))
        k_ref[g] = k.reshape(ts, p, p)


def _s5_lag_kernels(cr, ci, bbr, bbi, pwr, pwi):
    g, p, n = cr.shape
    ts = pwr.shape[1] - 1
    gb = S5_LANE_GROUPS
    cspec = pl.BlockSpec((gb, p, n), lambda i: (i, 0, 0))
    bspec = pl.BlockSpec((gb, n, p), lambda i: (i, 0, 0))
    pspec = pl.BlockSpec((gb, ts + 1, n), lambda i: (i, 0, 0))
    return pl.pallas_call(
        _s5_lag_kernel,
        grid=(g // gb,),
        in_specs=[cspec, cspec, bspec, bspec, pspec, pspec],
        out_specs=pl.BlockSpec((gb, ts, p, p), lambda i: (i, 0, 0, 0)),
        out_shape=jax.ShapeDtypeStruct((g, ts, p, p), F32),
        compiler_params=_params("parallel"),
        name="s5_lag_kernels",
    )(cr, ci, bbr, bbi, pwr, pwi)


def _s5_operators(a_re, a_im, b_re, b_im, c_re, c_im, log_dt):
    ts, gl = S5_SUB, S5_LANE_GROUPS
    g, n = a_re.shape
    p = b_re.shape[-1]
    nc = g // gl
    dt = jnp.exp(log_dt)[:, None]
    lre, lim = dt * a_re, dt * a_im
    mag = jnp.exp(lre)
    abar_re, abar_im = mag * jnp.cos(lim), mag * jnp.sin(lim)
    den = a_re * a_re + a_im * a_im
    z_re = ((abar_re - 1.0) * a_re + abar_im * a_im) / den
    z_im = (abar_im * a_re - (abar_re - 1.0) * a_im) / den
    bb_re = z_re[..., None] * b_re - z_im[..., None] * b_im
    bb_im = z_re[..., None] * b_im + z_im[..., None] * b_re
    steps = jnp.arange(ts + 1, dtype=F32)[None, :, None]
    pmag = jnp.exp(steps * lre[:, None, :])
    pw_re = pmag * jnp.cos(steps * lim[:, None, :])
    pw_im = pmag * jnp.sin(steps * lim[:, None, :])

    k = _s5_lag_kernels(c_re, c_im, bb_re, bb_im, pw_re, pw_im)
    bd = jnp.swapaxes(k, 2, 3).reshape(nc, gl, ts, p, p).transpose(0, 2, 1, 3, 4).reshape(nc, ts, gl * p, p)

    rev_re, rev_im = pw_re[:, ts - 1::-1][:, :ts], pw_im[:, ts - 1::-1][:, :ts]
    bt_re, bt_im = jnp.swapaxes(bb_re, 1, 2), jnp.swapaxes(bb_im, 1, 2)
    sb_re = rev_re[:, :, None, :] * bt_re[:, None] - rev_im[:, :, None, :] * bt_im[:, None]
    sb_im = rev_re[:, :, None, :] * bt_im[:, None] + rev_im[:, :, None, :] * bt_re[:, None]

    def rows_sgq(a):
        a = a.reshape(nc, gl, ts, p, n).transpose(0, 2, 1, 3, 4).reshape(nc, ts * gl * p, n)
        return jnp.concatenate([a] * (LANES // n), axis=-1)
    s_re2, s_im2 = rows_sgq(sb_re), rows_sgq(sb_im)

    c4 = lambda a: a.reshape(nc, gl, p, n).transpose(0, 3, 1, 2)[:, :, None]
    p4 = lambda a: a[:, 1:].reshape(nc, gl, ts, n).transpose(0, 3, 2, 1)[..., None]
    ct_re = c4(c_re) * p4(pw_re) - c4(c_im) * p4(pw_im)
    ct_im = c4(c_re) * p4(pw_im) + c4(c_im) * p4(pw_re)
    cwc = jnp.stack([ct_re, -ct_im], axis=1).reshape(nc, 2, n, ts * gl * p)
    return bd, s_re2, s_im2, cwc, (lre, lim)


def _s5_wpow(lre, lim, n_pow):
    ts, gl = S5_SUB, S5_LANE_GROUPS
    g, n = lre.shape
    nc = g // gl
    e = (ts * (2.0 ** jnp.arange(n_pow, dtype=F32)))[None, :, None]
    mag = jnp.exp(e * lre[:, None, :])
    wr = (mag * jnp.cos(e * lim[:, None, :])).reshape(nc, gl, n_pow, n)
    wi = (mag * jnp.sin(e * lim[:, None, :])).reshape(nc, gl, n_pow, n)
    w = jnp.stack([wr, wi], axis=0)
    return jnp.transpose(w, (1, 3, 0, 2, 4)).reshape(nc, n_pow, 2 * gl * n)


def _s5_assemble(bd_ref, sre_ref, sim_ref, cw_ref, m_scr, s_scr, c_scr):
    ts = bd_ref.shape[1]
    n = cw_ref.shape[2]
    half = s_scr.shape[1] // 2
    gl = half // n
    p = LANES // gl
    square = (LANES, LANES)
    spread = (lax.broadcasted_iota(jnp.int32, (p, LANES), 1) % p == lax.broadcasted_iota(jnp.int32, (p, LANES), 0))
    spread = jnp.where(spread, 1.0, 0.0).astype(m_scr.dtype)
    same_group = lax.broadcasted_iota(jnp.int32, square, 0) // p == lax.broadcasted_iota(jnp.int32, square, 1) // p
    lag_blocks = [jnp.where(same_group, jnp.dot(bd_ref[0, j].astype(m_scr.dtype), spread, preferred_element_type=F32),
                            0.0).astype(m_scr.dtype) for j in range(ts)]
    for s in range(ts):
        for t in range(ts):
            blk = (slice(s * LANES, (s + 1) * LANES), slice(t * LANES, (t + 1) * LANES))
            if t >= s:
                m_scr[blk] = lag_blocks[t - s]
            elif t * LANES // MXU_DIM == s * LANES // MXU_DIM:
                m_scr[blk] = jnp.zeros((LANES, LANES), m_scr.dtype)
    shape = (s_scr.shape[0], half)
    own = ((lax.broadcasted_iota(jnp.int32, shape, 0) // p) % gl) == (lax.broadcasted_iota(jnp.int32, shape, 1) // n)
    for ref, lo in ((sre_ref, 0), (sim_ref, half)):
        wide = jnp.concatenate([ref[0]] * (half // LANES), axis=1)
        s_scr[:, lo:lo + half] = jnp.where(own, wide, 0.0).astype(s_scr.dtype)
    colg = (lax.broadcasted_iota(jnp.int32, (n, c_scr.shape[1]), 1) // p) % gl
    for r in range(2):
        for g in range(gl):
            c_scr[r * half + g * n:r * half + (g + 1) * n, :] = jnp.where(colg == g, cw_ref[0, r], 0.0).astype(c_scr.dtype)


def _s5_state_scan(ds, w_ref, ds_scr, hs_scr, n_pow):
    rows, width = ds.shape
    half = width // 2
    nct = half // LANES
    ng = rows // SUBLANES
    assert rows <= 2 ** n_pow and rows % SUBLANES == 0
    for ct in range(2 * nct):
        ds_scr[ct] = ds[:, ct * LANES:(ct + 1) * LANES]
    every = lambda s: pl.ds(s, ng, stride=SUBLANES)
    cmul = lambda ar, ai, br, bi: (ar * br - ai * bi, ar * bi + ai * br)
    for ct in range(nct):
        cols = slice(ct * LANES, (ct + 1) * LANES)
        icols = slice(half + ct * LANES, half + (ct + 1) * LANES)
        w1r, w1i = w_ref[0, 0:1, cols], w_ref[0, 0:1, icols]
        hloc = [(ds_scr[ct, every(0), :], ds_scr[nct + ct, every(0), :])]
        for s in range(1, SUBLANES):
            pr, pi = cmul(w1r, w1i, *hloc[-1])
            hloc.append((pr + ds_scr[ct, every(s), :], pi + ds_scr[nct + ct, every(s), :]))
        gr, gi = hloc[-1]
        d, k = 1, 3
        while d < ng:
            sr, si = cmul(w_ref[0, k:k + 1, cols], w_ref[0, k:k + 1, icols],
                          _shift_rows(gr, d, 0.0), _shift_rows(gi, d, 0.0))
            gr, gi = gr + sr, gi + si
            d, k = 2 * d, k + 1
        cr, ci = _shift_rows(gr, 1, 0.0), _shift_rows(gi, 1, 0.0)
        hs_scr[ct, every(0), :] = cr
        hs_scr[nct + ct, every(0), :] = ci
        pr, pi = w1r, w1i
        for s in range(1, SUBLANES):
            ar, ai = cmul(pr, pi, cr, ci)
            hs_scr[ct, every(s), :] = hloc[s - 1][0] + ar
            hs_scr[nct + ct, every(s), :] = hloc[s - 1][1] + ai
            if s + 1 < SUBLANES:
                pr, pi = cmul(pr, pi, w1r, w1i)
    return jnp.concatenate([hs_scr[ct] for ct in range(2 * nct)], axis=1)


def _s5_kernel(x_ref, bd_ref, sre_ref, sim_ref, cw_ref, w_ref, d_ref, y_ref, m_scr, s_scr, c_scr, ds_scr, hs_scr,
               *, n_pow):
    @pl.when(pl.program_id(1) == 0)
    def _():
        _s5_assemble(bd_ref, sre_ref, sim_ref, cw_ref, m_scr, s_scr, c_scr)

    x = x_ref[0]
    ds = jnp.dot(x, s_scr[...], preferred_element_type=F32)
    hs = _s5_state_scan(ds, w_ref, ds_scr, hs_scr, n_pow).astype(BF16)
    for j in range(x.shape[1] // MXU_DIM):
        lo, hi_ = j * MXU_DIM, (j + 1) * MXU_DIM
        y = (jnp.dot(x[:, :hi_], m_scr[:hi_, lo:hi_], preferred_element_type=F32)
             + jnp.dot(hs, c_scr[:, lo:hi_], preferred_element_type=F32))
        y = y + d_ref[0, :, lo:hi_] * x[:, lo:hi_].astype(F32)
        y_ref[0, :, lo:hi_] = _gelu(y).astype(y_ref.dtype)


def _s5_branch(u2, layer, ops, bsz):
    nc, n_sub, width = u2.shape
    rows = n_sub // bsz
    cwc, wpow = ops[3], ops[4]
    n_state2 = 2 * S5_LANE_GROUPS * cwc.shape[3]
    x_spec = pl.BlockSpec((1, rows, width), lambda c, b: (c, b, 0))
    op_spec = lambda a: pl.BlockSpec((None, 1) + a.shape[2:], lambda c, b: (layer, c) + (0,) * (a.ndim - 2))
    return pl.pallas_call(
        functools.partial(_s5_kernel, n_pow=wpow.shape[2]),
        grid=(nc, bsz),
        in_specs=[x_spec] + [op_spec(a) for a in ops],
        out_specs=x_spec,
        out_shape=jax.ShapeDtypeStruct(u2.shape, BF16),
        scratch_shapes=[pltpu.VMEM((width, width), BF16), pltpu.VMEM((width, n_state2), BF16),
                        pltpu.VMEM((n_state2, width), BF16)]
                       + [pltpu.VMEM((n_state2 // LANES, rows, LANES), F32)] * 2,
        compiler_params=_params("arbitrary", "arbitrary"),
        name="s5_branch",
    )(u2, *ops)


def _route(logits_t, bias):
    e, tm = logits_t.shape
    per = e // N_EXPERT_GROUPS
    scores = _sigmoid(logits_t)
    biased = scores + bias
    g3 = biased.reshape(N_EXPERT_GROUPS, per, tm)
    sub = lax.broadcasted_iota(jnp.int32, g3.shape, 1)
    m1 = jnp.max(g3, axis=1, keepdims=True)
    first = jnp.min(jnp.where(g3 == m1, sub, per), axis=1, keepdims=True)
    m2 = jnp.max(jnp.where(sub == first, -jnp.inf, g3), axis=1, keepdims=True)
    gs = (m1 + m2).reshape(N_EXPERT_GROUPS, tm)
    gidx = lax.broadcasted_iota(jnp.int32, gs.shape, 0)
    grank = jnp.zeros(gs.shape, jnp.int32)
    for j in range(N_EXPERT_GROUPS):
        other = gs[j:j + 1, :]
        ahead = jnp.logical_or(other > gs, jnp.logical_and(other == gs, j < gidx))
        grank = grank + ahead.astype(jnp.int32)
    gsel = (grank < TOPK_GROUPS).reshape(N_EXPERT_GROUPS, 1, tm)
    masked = jnp.where(gsel, g3, -jnp.inf).reshape(e, tm)
    eidx = lax.broadcasted_iota(jnp.int32, masked.shape, 0)
    work, sel = masked, None
    for _ in range(TOP_K):
        top = jnp.max(work, axis=0, keepdims=True)
        cand = work == top if sel is None else jnp.logical_and(work == top, jnp.logical_not(sel))
        first = jnp.min(jnp.where(cand, eidx, e), axis=0, keepdims=True)
        pick = eidx == first
        sel = pick if sel is None else jnp.logical_or(sel, pick)
        work = jnp.where(pick, -jnp.inf, work)
    gate = jnp.where(sel, scores, 0.0)
    denom = jnp.sum(gate, axis=0, keepdims=True)
    return ROUTED_SCALE * gate / denom


def _mixout_kernel(ys_ref, ma_ref, gb_ref, x_ref, wglu_ref, bglu_ref, wout_ref, g_ref, b_ref, rw_ref, rb_ref,
                   x1_ref, comb_ref, ys_scr, *, alpha):
    d = x_ref.shape[1]
    nc, n_sub, width = ys_ref.shape
    ts = width // LANES
    for c in range(nc):
        for s in range(ts):
            ys_scr[c, pl.ds(s, n_sub, stride=ts), :] = ys_ref[c, :, s * LANES:(s + 1) * LANES].astype(F32)
    ys = jnp.concatenate([ys_scr[c] for c in range(nc)], axis=1).astype(BF16)
    glu = jnp.dot(ys, wglu_ref[...], preferred_element_type=F32) + bglu_ref[...]
    yb = glu[:, :d] * _sigmoid(glu[:, d:])
    merged = ma_ref[...].astype(F32) + _sigmoid(gb_ref[...].astype(F32)) * yb
    mix = jnp.dot(merged.astype(BF16), wout_ref[...], preferred_element_type=F32)
    x1 = _layer_norm(alpha * x_ref[...] + mix, g_ref[...], b_ref[...])
    x1_ref[...] = x1
    rw = rw_ref[...]
    rw_hi = rw.astype(BF16)
    rw_lo = (rw - rw_hi.astype(F32)).astype(BF16)
    x_hi = x1.astype(BF16)
    x_lo = (x1 - x_hi.astype(F32)).astype(BF16)
    nt = (((1,), (1,)), ((), ()))
    logits_t = (lax.dot_general(rw_hi, x_hi, nt, preferred_element_type=F32)
                + lax.dot_general(rw_hi, x_lo, nt, preferred_element_type=F32)
                + lax.dot_general(rw_lo, x_hi, nt, preferred_element_type=F32))
    comb_ref[...] = _route(logits_t, rb_ref[...]).T


def _mixout(ys, ma, gb, x, layer, params, alpha, tm):
    t, d = x.shape
    e = params[5].shape[1]
    nc, _, width = ys.shape
    ts = width // LANES
    row = lambda i: (i, 0)
    return pl.pallas_call(
        functools.partial(_mixout_kernel, alpha=alpha),
        grid=(t // tm,),
        in_specs=[pl.BlockSpec((nc, tm // ts, width), lambda i: (0, i, 0))] +
                 [pl.BlockSpec((tm, d), row)] * 3 + [_layer_spec(a, layer) for a in params],
        out_specs=[pl.BlockSpec((tm, d), row), pl.BlockSpec((tm, e), row)],
        out_shape=[jax.ShapeDtypeStruct((t, d), F32), jax.ShapeDtypeStruct((t, e), F32)],
        scratch_shapes=[pltpu.VMEM((nc, tm, LANES), F32)],
        compiler_params=_params("parallel"),
        name="mixout",
    )(ys, ma, gb, x, *params)


def _moe_kernel(x_ref, comb_ref, wg_ref, wu_ref, wd_ref, sg_ref, su_ref, sd_ref, p_ref, pw_ref, pgw_ref, pgb_ref,
                g_ref, b_ref, o_ref, xb_ref, *, alpha):
    c = pl.program_id(1)
    ec = wg_ref.shape[0]

    half = o_ref.shape[1] // 2

    def add_ffn(xb, wg, wu, wd_ref_e, scale):
        gg = jnp.dot(xb, wg, preferred_element_type=F32)
        hdn = gg * _sigmoid(gg) * jnp.dot(xb, wu, preferred_element_type=F32)
        if scale is not None:
            hdn = hdn * scale
        hdn = hdn.astype(BF16)
        for lo in (0, half):
            o_ref[:, lo:lo + half] += jnp.dot(hdn, wd_ref_e[:, lo:lo + half], preferred_element_type=F32)

    @pl.when(c == 0)
    def _():
        x = x_ref[...]
        xb = x.astype(BF16)
        xb_ref[...] = xb
        gate = _sigmoid(jnp.dot(xb, pgw_ref[...], preferred_element_type=F32) + pgb_ref[...])
        ple = gate * jnp.dot(p_ref[...].astype(BF16), pw_ref[...], preferred_element_type=F32)
        o_ref[...] = alpha * x + ple
        add_ffn(xb, sg_ref[...], su_ref[...], sd_ref, None)

    xb = xb_ref[...]
    comb = comb_ref[...]
    lane = lax.broadcasted_iota(jnp.int32, comb.shape, 1)
    for e in range(ec):
        col = jnp.sum(jnp.where(lane == c * ec + e, comb, 0.0), axis=1, keepdims=True)
        add_ffn(xb, wg_ref[e], wu_ref[e], wd_ref.at[e], col)

    @pl.when(c == pl.num_programs(1) - 1)
    def _():
        o_ref[...] = _layer_norm(o_ref[...], g_ref[...], b_ref[...])


def _moe(x1, comb, layer, wg, wu, wd, sg, su, sd, p, pw, pgw, pgb, g, b, alpha, tm, ec):
    t, d = x1.shape
    _, e, _, f = wg.shape
    dp = p.shape[2]
    expert = lambda i, c: (layer, c, 0, 0)
    const = lambda a: _layer_spec(a, layer, single_buffer=True)
    return pl.pallas_call(
        functools.partial(_moe_kernel, alpha=alpha),
        grid=(t // tm, e // ec),
        in_specs=[pl.BlockSpec((tm, d), lambda i, c: (i, 0)),
                  pl.BlockSpec((tm, e), lambda i, c: (i, 0)),
                  pl.BlockSpec((None, ec, d, f), expert),
                  pl.BlockSpec((None, ec, d, f), expert),
                  pl.BlockSpec((None, ec, f, d), expert),
                  const(sg), const(su), const(sd),
                  pl.BlockSpec((None, tm, dp), lambda i, c: (layer, i, 0)),
                  const(pw), const(pgw), const(pgb), const(g), const(b)],
        out_specs=pl.BlockSpec((tm, d), lambda i, c: (i, 0)),
        out_shape=jax.ShapeDtypeStruct((t, d), F32),
        scratch_shapes=[pltpu.VMEM((tm, d), BF16)],
        compiler_params=_params("parallel", "arbitrary"),
        name="moe",
    )(x1, comb, wg, wu, wd, sg, su, sd, p, pw, pgw, pgb, g, b)


def _pick_tile(n, target):
    tm = min(n, target)
    assert n % tm == 0, (n, tm)
    return tm


def kernel(x, p, w_in, conv_w, conv_b, lru_wa, lru_ba, lru_wx, lru_bx, lru_lambda, w_lru_out, s5_a_re, s5_a_im, s5_b_re, s5_b_im, s5_c_re, s5_c_im, s5_d, s5_log_dt, w_glu, b_glu, w_out, ln1_g, ln1_b, router_w, router_bias, moe_w_gate, moe_w_up, moe_w_down, shared_w_gate, shared_w_up, shared_w_down, ple_w, ple_gate_w, ple_gate_b, ln2_g, ln2_b):
    depth = w_in.shape[0]
    bsz, seq, d = x.shape
    t = bsz * seq
    d_rnn = conv_w.shape[-1]
    d_s5 = s5_d.shape[-1]
    alpha = (2.0 * depth) ** 0.25
    widths = (d_rnn, d_rnn, d_s5, d, d)
    assert sum(widths) == w_in.shape[-1]
    assert d_s5 % LANES == 0 and seq % S5_SUB == 0

    heads, dh = lru_wa.shape[1], lru_wa.shape[2]
    hpb = (LANES // math.gcd(dh, LANES))
    assert heads % hpb == 0
    ec = MOE_EXPERTS_PER_STEP
    tm_lru = _pick_tile(seq, TM_SEQ)
    tm_mix = _pick_tile(t, TM_MIX)
    tm_moe = _pick_tile(t, TM_MOE)
    n_pow = max(1, int(math.ceil(math.log2(seq // S5_SUB))))

    bf = lambda a: a.astype(BF16)
    w_in_b, w_lru_out_b, w_glu_b, w_out_b = bf(w_in), bf(w_lru_out), bf(w_glu), bf(w_out)
    wg_b, wu_b, wd_b = bf(moe_w_gate), bf(moe_w_up), bf(moe_w_down)
    sg_b, su_b, sd_b = bf(shared_w_gate), bf(shared_w_up), bf(shared_w_down)
    ple_w_b, ple_gate_w_b = bf(ple_w), bf(ple_gate_w)
    lru_params = (conv_w, _row3(conv_b), _block_diag_heads(lru_wa, hpb), _row3(lru_ba.reshape(depth, -1)),
                  _block_diag_heads(lru_wx, hpb), _row3(lru_bx.reshape(depth, -1)), _row3(lru_lambda), w_lru_out_b)
    mix_params = (w_glu_b, _row3(b_glu), w_out_b, _row3(ln1_g), _row3(ln1_b),
                  jnp.swapaxes(router_w, 1, 2), router_bias[..., None])

    groups, n_state = s5_a_re.shape[1], s5_a_re.shape[2]
    nc = d_s5 // LANES
    fold = lambda a: a.reshape((depth * groups,) + a.shape[2:])
    bd, s_re2, s_im2, cwc, (lre, lim) = _s5_operators(fold(s5_a_re), fold(s5_a_im), fold(s5_b_re), fold(s5_b_im),
                                                     fold(s5_c_re), fold(s5_c_im), fold(s5_log_dt))
    unfold = lambda a: a.reshape((depth, nc) + a.shape[1:])
    d_t = jnp.tile(s5_d.reshape(depth, nc, 1, LANES), (1, 1, 1, S5_SUB))
    s5_ops = (unfold(bd), unfold(s_re2), unfold(s_im2), unfold(cwc), unfold(_s5_wpow(lre, lim, n_pow)), d_t)
    assert groups == nc * S5_LANE_GROUPS and n_state * (LANES // n_state) == LANES

    h = x.reshape(t, d)
    p2 = p.reshape(depth, t, p.shape[-1])
    for i in range(depth):
        u_s5, gate_b, m_a = _inproj_lru(h, i, w_in_b, lru_params, widths, S5_SUB, bsz, tm_lru)
        ys = _s5_branch(u_s5, i, s5_ops, bsz)
        x1, comb = _mixout(ys, m_a, gate_b, h, i, mix_params, alpha, tm_mix)
        h = _moe(x1, comb, i, wg_b, wu_b, wd_b, sg_b, su_b, sd_b, p2, ple_w_b, ple_gate_w_b, _row3(ple_gate_b),
                 _row3(ln2_g), _row3(ln2_b), alpha, tm_moe, ec)
    return h.reshape(bsz, seq, d)
```

```python
import functools
import math

import jax
import jax.numpy as jnp
from jax import lax
from jax.experimental import pallas as pl
from jax.experimental.pallas import tpu as pltpu

F32 = jnp.float32
BF16 = jnp.bfloat16

LRU_C = 8.0
TOP_K = 8
N_EXPERT_GROUPS = 8
TOPK_GROUPS = 4
ROUTED_SCALE = 2.5
LN_EPS = 1e-5

LANES = 128
SUBLANES = 8
MXU_DIM = 256
PROJ_CHUNK = 2 * MXU_DIM
VMEM_LIMIT_BYTES = 60 * 1024 * 1024

S5_SUB = 16
S5_LANE_GROUPS = 8

TM_SEQ = 256
TM_MIX = 512
TM_MOE = 1024
MOE_EXPERTS_PER_STEP = 4


def _params(*sem):
    return pltpu.CompilerParams(dimension_semantics=sem, vmem_limit_bytes=VMEM_LIMIT_BYTES)


def _layer_spec(stacked, layer, single_buffer=False):
    nd = stacked.ndim - 1
    mode = {"pipeline_mode": pl.Buffered(1)} if single_buffer else {}
    return pl.BlockSpec((None,) + stacked.shape[1:], lambda *_: (layer,) + (0,) * nd, **mode)


def _row3(v):
    return v.reshape(v.shape[0], 1, -1)


def _gelu(x):
    return 0.5 * x * (1.0 + jnp.tanh(math.sqrt(2.0 / math.pi) * (x + 0.044715 * (x * x * x))))


def _sigmoid(x):
    return 1.0 / (1.0 + jnp.exp(-x))


def _layer_norm(v, g, b):
    mu = jnp.mean(v, axis=-1, keepdims=True)
    c = v - mu
    var = jnp.mean(c * c, axis=-1, keepdims=True)
    return c * lax.rsqrt(var + LN_EPS) * g + b


def _shift_rows(v, d, fill):
    n = v.shape[0]
    if d % SUBLANES == 0:
        head = jnp.broadcast_to(jnp.asarray(fill, v.dtype), (d,) + v.shape[1:])
        return jnp.concatenate([head, v[:n - d]], axis=0)
    row = lax.broadcasted_iota(jnp.int32, v.shape, 0)
    return jnp.where(row >= d, pltpu.roll(v, d, 0), fill)


def _scan_rows(a, b, h0, a_scr, b_scr, h_scr):
    n, c = a.shape
    ng = n // SUBLANES
    for ct in range(c // LANES):
        a_scr[ct] = a[:, ct * LANES:(ct + 1) * LANES]
        b_scr[ct] = b[:, ct * LANES:(ct + 1) * LANES]
    for ct in range(c // LANES):
        every = lambda s: pl.ds(s, ng, stride=SUBLANES)
        a_s = [a_scr[ct, every(s), :] for s in range(SUBLANES)]
        acum, hloc = [a_s[0]], [b_scr[ct, every(0), :]]
        for s in range(1, SUBLANES):
            hloc.append(a_s[s] * hloc[-1] + b_scr[ct, every(s), :])
            acum.append(a_s[s] * acum[-1])
        ga, gh = acum[-1], hloc[-1]
        d = 1
        while d < ng:
            gh = gh + ga * _shift_rows(gh, d, 0.0)
            ga = ga * _shift_rows(ga, d, 1.0)
            d *= 2
        h0c = h0[:, ct * LANES:(ct + 1) * LANES]
        after = gh + ga * h0c
        cin = _shift_rows(after, 1, h0c)
        for s in range(SUBLANES):
            h_scr[ct, every(s), :] = hloc[s] + acum[s] * cin
    return jnp.concatenate([h_scr[ct] for ct in range(c // LANES)], axis=1)


def _lru_tile(x_ref, g_ref, ga_ref, first, cw_ref, cb_ref, wa_ref, ba_ref, wx_ref, bx_ref, lam_ref, wo_ref,
              prev_ref, h_ref, a_scr, b_scr, hs_scr, n_blk, blk, side_work):
    sites = 5
    per_phase = -(-len(side_work) // sites)

    def run_side_work():
        for _ in range(min(per_phase, len(side_work))):
            side_work.pop(0)()

    tm = x_ref.shape[0]
    run_side_work()
    x = x_ref[...].astype(F32)
    ext = jnp.concatenate([prev_ref[...], x], axis=0)
    prev_ref[...] = x[tm - SUBLANES:, :]
    kw = cw_ref.shape[0]
    xc = cb_ref[...] + cw_ref[kw - 1:kw, :] * x
    for j in range(1, kw):
        xc = xc + cw_ref[kw - 1 - j:kw - j, :] * pltpu.roll(ext, j, 0)[SUBLANES:, :]

    run_side_work()
    xcb = xc.astype(BF16)
    r_parts, i_parts = [], []
    for k in range(n_blk):
        xk = xcb[:, k * blk:(k + 1) * blk]
        r_parts.append(jnp.dot(xk, wa_ref[k], preferred_element_type=F32))
        i_parts.append(jnp.dot(xk, wx_ref[k], preferred_element_type=F32))
    r = _sigmoid(jnp.concatenate(r_parts, axis=1) + ba_ref[...])
    ig = _sigmoid(jnp.concatenate(i_parts, axis=1) + bx_ref[...])

    run_side_work()
    nl = -lam_ref[...]
    softplus = jnp.maximum(nl, 0.0) + jnp.log1p(jnp.exp(-jnp.abs(nl)))
    log_a = (-LRU_C) * r * softplus
    a = jnp.exp(log_a)
    v = 1.0 - a * a
    mult = jnp.where(v > 0.0, v * lax.rsqrt(v), 0.0)
    row = lax.broadcasted_iota(jnp.int32, a.shape, 0)
    mult = jnp.where(jnp.logical_and(row == 0, first), 1.0, mult)
    b = mult * (ig * xc)

    run_side_work()
    h = _scan_rows(a, b, h_ref[0:1, :], a_scr, b_scr, hs_scr)
    h_ref[...] = jnp.broadcast_to(h[tm - 1:tm, :], h_ref.shape)

    run_side_work()
    y = (_gelu(g_ref[...].astype(F32)) * h).astype(BF16)
    ya = jnp.dot(y, wo_ref[...], preferred_element_type=F32)
    out = _sigmoid(ga_ref[...].astype(F32)) * ya
    while side_work:
        side_work.pop(0)()
    return out


def _inlru_kernel(x_ref, w_ref, cw_ref, cb_ref, wa_ref, ba_ref, wx_ref, bx_ref, lam_ref, wo_ref,
                  u_ref, gb_ref, ma_ref,
                  xl_scr, gl_scr, ga_scr, z_scr, prev_ref, h_ref, a_scr, b_scr, hs_scr, *, bounds, n_blk, blk):
    j = pl.program_id(1)

    @pl.when(j == 0)
    def _():
        xl_scr[...] = jnp.zeros_like(xl_scr)
        gl_scr[...] = jnp.zeros_like(gl_scr)
        ga_scr[...] = jnp.zeros_like(ga_scr)

    @pl.when(j <= 1)
    def _():
        prev_ref[...] = jnp.zeros_like(prev_ref)
        h_ref[...] = jnp.zeros_like(h_ref)

    fill, drain = j % 2, (j + 1) % 2
    xb = x_ref[...].astype(BF16)
    ts = u_ref.shape[2] // LANES

    def project(lo, hi, store):
        def thunk():
            store(jnp.dot(xb, w_ref[:, lo:hi], preferred_element_type=F32))
        return thunk

    def to_stash(scr, off):
        def store(z):
            scr[fill, :, off:off + z.shape[1]] = z.astype(scr.dtype)
        return store

    def to_gate_b(off):
        def store(z):
            gb_ref[:, off:off + z.shape[1]] = z.astype(gb_ref.dtype)
        return store

    def to_s5(off):
        def store(z):
            for k in range(z.shape[1] // LANES):
                c = off // LANES + k
                z_scr[c] = z[:, k * LANES:(k + 1) * LANES]
                for s in range(ts):
                    u_ref[c, :, s * LANES:(s + 1) * LANES] = z_scr[
                        c, pl.ds(s, u_ref.shape[1], stride=ts), :].astype(u_ref.dtype)
        return store

    sinks = (functools.partial(to_stash, xl_scr), functools.partial(to_stash, gl_scr), to_s5,
             functools.partial(to_stash, ga_scr), to_gate_b)
    side_work = []
    for (lo, hi), sink in zip(bounds, sinks):
        for c0 in range(lo, hi, PROJ_CHUNK):
            side_work.append(project(c0, min(c0 + PROJ_CHUNK, hi), sink(c0 - lo)))

    ma = _lru_tile(xl_scr.at[drain], gl_scr.at[drain], ga_scr.at[drain], j == 1,
                   cw_ref, cb_ref, wa_ref, ba_ref, wx_ref, bx_ref, lam_ref, wo_ref,
                   prev_ref, h_ref, a_scr, b_scr, hs_scr, n_blk, blk, side_work)
    ma_ref[...] = ma.astype(ma_ref.dtype)


def _inproj_lru(x, layer, w_all, lru_params, widths, ts, bsz, tm):
    t, d = x.shape
    d_rnn, _, d_s5, d_a, d_b = widths
    seq = t // bsz
    nt = seq // tm
    _, n_blk, blk, _ = lru_params[2].shape
    bounds, lo = [], 0
    for w in widths:
        bounds.append((lo, lo + w))
        lo += w
    cur = lambda b, j: (b * nt + jnp.minimum(j, nt - 1), 0)
    prv = lambda b, j: (b * nt + jnp.maximum(j - 1, 0), 0)
    nc = d_s5 // LANES
    return pl.pallas_call(
        functools.partial(_inlru_kernel, bounds=tuple(bounds), n_blk=n_blk, blk=blk),
        grid=(bsz, nt + 1),
        in_specs=[pl.BlockSpec((tm, d), cur), _layer_spec(w_all, layer, single_buffer=True)]
                 + [_layer_spec(a, layer) for a in lru_params],
        out_specs=[pl.BlockSpec((nc, tm // ts, ts * LANES), lambda b, j: (0,) + cur(b, j)),
                   pl.BlockSpec((tm, d_b), cur), pl.BlockSpec((tm, d_a), prv)],
        out_shape=[jax.ShapeDtypeStruct((nc, t // ts, ts * LANES), BF16),
                   jax.ShapeDtypeStruct((t, d_b), BF16), jax.ShapeDtypeStruct((t, d_a), BF16)],
        scratch_shapes=[pltpu.VMEM((2, tm, d_rnn), BF16), pltpu.VMEM((2, tm, d_rnn), BF16),
                        pltpu.VMEM((2, tm, d_a), BF16),
                        pltpu.VMEM((nc, tm, LANES), F32),
                        pltpu.VMEM((SUBLANES, d_rnn), F32), pltpu.VMEM((SUBLANES, d_rnn), F32)]
                       + [pltpu.VMEM((d_rnn // LANES, tm, LANES), F32)] * 3,
        compiler_params=_params("arbitrary", "arbitrary"),
        name="inproj_lru",
    )(x, w_all, *lru_params)


def _block_diag_heads(w, heads_per_blk):
    l, h, dh, _ = w.shape
    nb = h // heads_per_blk
    eye = jnp.eye(heads_per_blk, dtype=w.dtype)
    wb = w.reshape(l * nb, heads_per_blk, dh, dh)
    out = jnp.einsum("nhij,hk->nhikj", wb, eye)
    return out.reshape(l, nb, heads_per_blk * dh, heads_per_blk * dh).astype(BF16)


def _s5_lag_kernel(cr_ref, ci_ref, br_ref, bi_ref, pr_ref, pi_ref, k_ref):
    gb, p, n = cr_ref.shape
    ts = k_ref.shape[1]
    for g in range(gb):
        cr, ci = cr_ref[g], ci_ref[g]
        lag = lambda ref, j: ref[g, j:j + 1, :]
        are = jnp.concatenate([cr * lag(pr_ref, j) - ci * lag(pi_ref, j) for j in range(ts)], axis=0)
        aim = jnp.concatenate([cr * lag(pi_ref, j) + ci * lag(pr_ref, j) for j in range(ts)], axis=0)
        k = (jnp.dot(are, br_ref[g], preferred_element_type=F32, precision=lax.Precision.HIGHEST)
             - jnp.dot(aim, bi_ref[g], preferred_element_type=F32, precision=lax.Precision.HIGHEST))
        k_ref[g] = k.reshape(ts, p, p)


def _s5_lag_kernels(cr, ci, bbr, bbi, pwr, pwi):
    g, p, n = cr.shape
    ts = pwr.shape[1] - 1
    gb = S5_LANE_GROUPS
    cspec = pl.BlockSpec((gb, p, n), lambda i: (i, 0, 0))
    bspec = pl.BlockSpec((gb, n, p), lambda i: (i, 0, 0))
    pspec = pl.BlockSpec((gb, ts + 1, n), lambda i: (i, 0, 0))
    return pl.pallas_call(
        _s5_lag_kernel,
        grid=(g // gb,),
        in_specs=[cspec, cspec, bspec, bspec, pspec, pspec],
        out_specs=pl.BlockSpec((gb, ts, p, p), lambda i: (i, 0, 0, 0)),
        out_shape=jax.ShapeDtypeStruct((g, ts, p, p), F32),
        compiler_params=_params("parallel"),
        name="s5_lag_kernels",
    )(cr, ci, bbr, bbi, pwr, pwi)


def _s5_operators(a_re, a_im, b_re, b_im, c_re, c_im, log_dt):
    ts, gl = S5_SUB, S5_LANE_GROUPS
    g, n = a_re.shape
    p = b_re.shape[-1]
    nc = g // gl
    dt = jnp.exp(log_dt)[:, None]
    lre, lim = dt * a_re, dt * a_im
    mag = jnp.exp(lre)
    abar_re, abar_im = mag * jnp.cos(lim), mag * jnp.sin(lim)
    den = a_re * a_re + a_im * a_im
    z_re = ((abar_re - 1.0) * a_re + abar_im * a_im) / den
    z_im = (abar_im * a_re - (abar_re - 1.0) * a_im) / den
    bb_re = z_re[..., None] * b_re - z_im[..., None] * b_im
    bb_im = z_re[..., None] * b_im + z_im[..., None] * b_re
    steps = jnp.arange(ts + 1, dtype=F32)[None, :, None]
    pmag = jnp.exp(steps * lre[:, None, :])
    pw_re = pmag * jnp.cos(steps * lim[:, None, :])
    pw_im = pmag * jnp.sin(steps * lim[:, None, :])

    k = _s5_lag_kernels(c_re, c_im, bb_re, bb_im, pw_re, pw_im)
    bd = jnp.swapaxes(k, 2, 3).reshape(nc, gl, ts, p, p).transpose(0, 2, 1, 3, 4).reshape(nc, ts, gl * p, p)

    rev_re, rev_im = pw_re[:, ts - 1::-1][:, :ts], pw_im[:, ts - 1::-1][:, :ts]
    bt_re, bt_im = jnp.swapaxes(bb_re, 1, 2), jnp.swapaxes(bb_im, 1, 2)
    sb_re = rev_re[:, :, None, :] * bt_re[:, None] - rev_im[:, :, None, :] * bt_im[:, None]
    sb_im = rev_re[:, :, None, :] * bt_im[:, None] + rev_im[:, :, None, :] * bt_re[:, None]

    def rows_sgq(a):
        a = a.reshape(nc, gl, ts, p, n).transpose(0, 2, 1, 3, 4).reshape(nc, ts * gl * p, n)
        return jnp.concatenate([a] * (LANES // n), axis=-1)
    s_re2, s_im2 = rows_sgq(sb_re), rows_sgq(sb_im)

    c4 = lambda a: a.reshape(nc, gl, p, n).transpose(0, 3, 1, 2)[:, :, None]
    p4 = lambda a: a[:, 1:].reshape(nc, gl, ts, n).transpose(0, 3, 2, 1)[..., None]
    ct_re = c4(c_re) * p4(pw_re) - c4(c_im) * p4(pw_im)
    ct_im = c4(c_re) * p4(pw_im) + c4(c_im) * p4(pw_re)
    cwc = jnp.stack([ct_re, -ct_im], axis=1).reshape(nc, 2, n, ts * gl * p)
    return bd, s_re2, s_im2, cwc, (lre, lim)


def _s5_wpow(lre, lim, n_pow):
    ts, gl = S5_SUB, S5_LANE_GROUPS
    g, n = lre.shape
    nc = g // gl
    e = (ts * (2.0 ** jnp.arange(n_pow, dtype=F32)))[None, :, None]
    mag = jnp.exp(e * lre[:, None, :])
    wr = (mag * jnp.cos(e * lim[:, None, :])).reshape(nc, gl, n_pow, n)
    wi = (mag * jnp.sin(e * lim[:, None, :])).reshape(nc, gl, n_pow, n)
    w = jnp.stack([wr, wi], axis=0)
    return jnp.transpose(w, (1, 3, 0, 2, 4)).reshape(nc, n_pow, 2 * gl * n)


def _s5_assemble(bd_ref, sre_ref, sim_ref, cw_ref, m_scr, s_scr, c_scr):
    ts = bd_ref.shape[1]
    n = cw_ref.shape[2]
    half = s_scr.shape[1] // 2
    gl = half // n
    p = LANES // gl
    square = (LANES, LANES)
    spread = (lax.broadcasted_iota(jnp.int32, (p, LANES), 1) % p == lax.broadcasted_iota(jnp.int32, (p, LANES), 0))
    spread = jnp.where(spread, 1.0, 0.0).astype(m_scr.dtype)
    same_group = lax.broadcasted_iota(jnp.int32, square, 0) // p == lax.broadcasted_iota(jnp.int32, square, 1) // p
    lag_blocks = [jnp.where(same_group, jnp.dot(bd_ref[0, j].astype(m_scr.dtype), spread, preferred_element_type=F32),
                            0.0).astype(m_scr.dtype) for j in range(ts)]
    for s in range(ts):
        for t in range(ts):
            blk = (slice(s * LANES, (s + 1) * LANES), slice(t * LANES, (t + 1) * LANES))
            if t >= s:
                m_scr[blk] = lag_blocks[t - s]
            elif t * LANES // MXU_DIM == s * LANES // MXU_DIM:
                m_scr[blk] = jnp.zeros((LANES, LANES), m_scr.dtype)
    shape = (s_scr.shape[0], half)
    own = ((lax.broadcasted_iota(jnp.int32, shape, 0) // p) % gl) == (lax.broadcasted_iota(jnp.int32, shape, 1) // n)
    for ref, lo in ((sre_ref, 0), (sim_ref, half)):
        wide = jnp.concatenate([ref[0]] * (half // LANES), axis=1)
        s_scr[:, lo:lo + half] = jnp.where(own, wide, 0.0).astype(s_scr.dtype)
    colg = (lax.broadcasted_iota(jnp.int32, (n, c_scr.shape[1]), 1) // p) % gl
    for r in range(2):
        for g in range(gl):
            c_scr[r * half + g * n:r * half + (g + 1) * n, :] = jnp.where(colg == g, cw_ref[0, r], 0.0).astype(c_scr.dtype)


def _s5_state_scan(ds, w_ref, ds_scr, hs_scr, n_pow):
    rows, width = ds.shape
    half = width // 2
    nct = half // LANES
    ng = rows // SUBLANES
    assert rows <= 2 ** n_pow and rows % SUBLANES == 0
    for ct in range(2 * nct):
        ds_scr[ct] = ds[:, ct * LANES:(ct + 1) * LANES]
    every = lambda s: pl.ds(s, ng, stride=SUBLANES)
    cmul = lambda ar, ai, br, bi: (ar * br - ai * bi, ar * bi + ai * br)
    for ct in range(nct):
        cols = slice(ct * LANES, (ct + 1) * LANES)
        icols = slice(half + ct * LANES, half + (ct + 1) * LANES)
        w1r, w1i = w_ref[0, 0:1, cols], w_ref[0, 0:1, icols]
        hloc = [(ds_scr[ct, every(0), :], ds_scr[nct + ct, every(0), :])]
        for s in range(1, SUBLANES):
            pr, pi = cmul(w1r, w1i, *hloc[-1])
            hloc.append((pr + ds_scr[ct, every(s), :], pi + ds_scr[nct + ct, every(s), :]))
        gr, gi = hloc[-1]
        d, k = 1, 3
        while d < ng:
            sr, si = cmul(w_ref[0, k:k + 1, cols], w_ref[0, k:k + 1, icols],
                          _shift_rows(gr, d, 0.0), _shift_rows(gi, d, 0.0))
            gr, gi = gr + sr, gi + si
            d, k = 2 * d, k + 1
        cr, ci = _shift_rows(gr, 1, 0.0), _shift_rows(gi, 1, 0.0)
        hs_scr[ct, every(0), :] = cr
        hs_scr[nct + ct, every(0), :] = ci
        pr, pi = w1r, w1i
        for s in range(1, SUBLANES):
            ar, ai = cmul(pr, pi, cr, ci)
            hs_scr[ct, every(s), :] = hloc[s - 1][0] + ar
            hs_scr[nct + ct, every(s), :] = hloc[s - 1][1] + ai
            if s + 1 < SUBLANES:
                pr, pi = cmul(pr, pi, w1r, w1i)
    return jnp.concatenate([hs_scr[ct] for ct in range(2 * nct)], axis=1)


def _s5_kernel(x_ref, bd_ref, sre_ref, sim_ref, cw_ref, w_ref, d_ref, y_ref, m_scr, s_scr, c_scr, ds_scr, hs_scr,
               *, n_pow):
    @pl.when(pl.program_id(1) == 0)
    def _():
        _s5_assemble(bd_ref, sre_ref, sim_ref, cw_ref, m_scr, s_scr, c_scr)

    x = x_ref[0]
    ds = jnp.dot(x, s_scr[...], preferred_element_type=F32)
    hs = _s5_state_scan(ds, w_ref, ds_scr, hs_scr, n_pow).astype(BF16)
    for j in range(x.shape[1] // MXU_DIM):
        lo, hi_ = j * MXU_DIM, (j + 1) * MXU_DIM
        y = (jnp.dot(x[:, :hi_], m_scr[:hi_, lo:hi_], preferred_element_type=F32)
             + jnp.dot(hs, c_scr[:, lo:hi_], preferred_element_type=F32))
        y = y + d_ref[0, :, lo:hi_] * x[:, lo:hi_].astype(F32)
        y_ref[0, :, lo:hi_] = _gelu(y).astype(y_ref.dtype)


def _s5_branch(u2, layer, ops, bsz):
    nc, n_sub, width = u2.shape
    rows = n_sub // bsz
    cwc, wpow = ops[3], ops[4]
    n_state2 = 2 * S5_LANE_GROUPS * cwc.shape[3]
    x_spec = pl.BlockSpec((1, rows, width), lambda c, b: (c, b, 0))
    op_spec = lambda a: pl.BlockSpec((None, 1) + a.shape[2:], lambda c, b: (layer, c) + (0,) * (a.ndim - 2))
    return pl.pallas_call(
        functools.partial(_s5_kernel, n_pow=wpow.shape[2]),
        grid=(nc, bsz),
        in_specs=[x_spec] + [op_spec(a) for a in ops],
        out_specs=x_spec,
        out_shape=jax.ShapeDtypeStruct(u2.shape, BF16),
        scratch_shapes=[pltpu.VMEM((width, width), BF16), pltpu.VMEM((width, n_state2), BF16),
                        pltpu.VMEM((n_state2, width), BF16)]
                       + [pltpu.VMEM((n_state2 // LANES, rows, LANES), F32)] * 2,
        compiler_params=_params("arbitrary", "arbitrary"),
        name="s5_branch",
    )(u2, *ops)


def _route(logits_t, bias):
    e, tm = logits_t.shape
    per = e // N_EXPERT_GROUPS
    scores = _sigmoid(logits_t)
    biased = scores + bias
    g3 = biased.reshape(N_EXPERT_GROUPS, per, tm)
    sub = lax.broadcasted_iota(jnp.int32, g3.shape, 1)
    m1 = jnp.max(g3, axis=1, keepdims=True)
    first = jnp.min(jnp.where(g3 == m1, sub, per), axis=1, keepdims=True)
    m2 = jnp.max(jnp.where(sub == first, -jnp.inf, g3), axis=1, keepdims=True)
    gs = (m1 + m2).reshape(N_EXPERT_GROUPS, tm)
    gidx = lax.broadcasted_iota(jnp.int32, gs.shape, 0)
    grank = jnp.zeros(gs.shape, jnp.int32)
    for j in range(N_EXPERT_GROUPS):
        other = gs[j:j + 1, :]
        ahead = jnp.logical_or(other > gs, jnp.logical_and(other == gs, j < gidx))
        grank = grank + ahead.astype(jnp.int32)
    gsel = (grank < TOPK_GROUPS).reshape(N_EXPERT_GROUPS, 1, tm)
    masked = jnp.where(gsel, g3, -jnp.inf).reshape(e, tm)
    eidx = lax.broadcasted_iota(jnp.int32, masked.shape, 0)
    work, sel = masked, None
    for _ in range(TOP_K):
        top = jnp.max(work, axis=0, keepdims=True)
        cand = work == top if sel is None else jnp.logical_and(work == top, jnp.logical_not(sel))
        first = jnp.min(jnp.where(cand, eidx, e), axis=0, keepdims=True)
        pick = eidx == first
        sel = pick if sel is None else jnp.logical_or(sel, pick)
        work = jnp.where(pick, -jnp.inf, work)
    gate = jnp.where(sel, scores, 0.0)
    denom = jnp.sum(gate, axis=0, keepdims=True)
    return ROUTED_SCALE * gate / denom


def _mixout_kernel(ys_ref, ma_ref, gb_ref, x_ref, wglu_ref, bglu_ref, wout_ref, g_ref, b_ref, rw_ref, rb_ref,
                   x1_ref, comb_ref, ys_scr, *, alpha):
    d = x_ref.shape[1]
    nc, n_sub, width = ys_ref.shape
    ts = width // LANES
    for c in range(nc):
        for s in range(ts):
            ys_scr[c, pl.ds(s, n_sub, stride=ts), :] = ys_ref[c, :, s * LANES:(s + 1) * LANES].astype(F32)
    ys = jnp.concatenate([ys_scr[c] for c in range(nc)], axis=1).astype(BF16)
    glu = jnp.dot(ys, wglu_ref[...], preferred_element_type=F32) + bglu_ref[...]
    yb = glu[:, :d] * _sigmoid(glu[:, d:])
    merged = ma_ref[...].astype(F32) + _sigmoid(gb_ref[...].astype(F32)) * yb
    mix = jnp.dot(merged.astype(BF16), wout_ref[...], preferred_element_type=F32)
    x1 = _layer_norm(alpha * x_ref[...] + mix, g_ref[...], b_ref[...])
    x1_ref[...] = x1
    rw = rw_ref[...]
    rw_hi = rw.astype(BF16)
    rw_lo = (rw - rw_hi.astype(F32)).astype(BF16)
    x_hi = x1.astype(BF16)
    x_lo = (x1 - x_hi.astype(F32)).astype(BF16)
    nt = (((1,), (1,)), ((), ()))
    logits_t = (lax.dot_general(rw_hi, x_hi, nt, preferred_element_type=F32)
                + lax.dot_general(rw_hi, x_lo, nt, preferred_element_type=F32)
                + lax.dot_general(rw_lo, x_hi, nt, preferred_element_type=F32))
    comb_ref[...] = _route(logits_t, rb_ref[...]).T


def _mixout(ys, ma, gb, x, layer, params, alpha, tm):
    t, d = x.shape
    e = params[5].shape[1]
    nc, _, width = ys.shape
    ts = width // LANES
    row = lambda i: (i, 0)
    return pl.pallas_call(
        functools.partial(_mixout_kernel, alpha=alpha),
        grid=(t // tm,),
        in_specs=[pl.BlockSpec((nc, tm // ts, width), lambda i: (0, i, 0))] +
                 [pl.BlockSpec((tm, d), row)] * 3 + [_layer_spec(a, layer) for a in params],
        out_specs=[pl.BlockSpec((tm, d), row), pl.BlockSpec((tm, e), row)],
        out_shape=[jax.ShapeDtypeStruct((t, d), F32), jax.ShapeDtypeStruct((t, e), F32)],
        scratch_shapes=[pltpu.VMEM((nc, tm, LANES), F32)],
        compiler_params=_params("parallel"),
        name="mixout",
    )(ys, ma, gb, x, *params)


def _moe_kernel(x_ref, comb_ref, wg_ref, wu_ref, wd_ref, sg_ref, su_ref, sd_ref, p_ref, pw_ref, pgw_ref, pgb_ref,
                g_ref, b_ref, o_ref, xb_ref, *, alpha):
    c = pl.program_id(1)
    ec = wg_ref.shape[0]

    half = o_ref.shape[1] // 2

    def add_ffn(xb, wg, wu, wd_ref_e, scale):
        gg = jnp.dot(xb, wg.astype(BF16), preferred_element_type=F32)
        hdn = gg * _sigmoid(gg) * jnp.dot(xb, wu.astype(BF16), preferred_element_type=F32)
        if scale is not None:
            hdn = hdn * scale
        hdn = hdn.astype(BF16)
        for lo in (0, half):
            o_ref[:, lo:lo + half] += jnp.dot(hdn, wd_ref_e[:, lo:lo + half].astype(BF16),
                                              preferred_element_type=F32)

    @pl.when(c == 0)
    def _():
        x = x_ref[...]
        xb = x.astype(BF16)
        xb_ref[...] = xb
        gate = _sigmoid(jnp.dot(xb, pgw_ref[...], preferred_element_type=F32) + pgb_ref[...])
        ple = gate * jnp.dot(p_ref[...].astype(BF16), pw_ref[...], preferred_element_type=F32)
        o_ref[...] = alpha * x + ple
        add_ffn(xb, sg_ref[...], su_ref[...], sd_ref, None)

    xb = xb_ref[...]
    comb = comb_ref[...]
    lane = lax.broadcasted_iota(jnp.int32, comb.shape, 1)
    for e in range(ec):
        col = jnp.sum(jnp.where(lane == c * ec + e, comb, 0.0), axis=1, keepdims=True)
        add_ffn(xb, wg_ref[e], wu_ref[e], wd_ref.at[e], col)

    @pl.when(c == pl.num_programs(1) - 1)
    def _():
        o_ref[...] = _layer_norm(o_ref[...], g_ref[...], b_ref[...])


def _moe(x1, comb, layer, wg, wu, wd, sg, su, sd, p, pw, pgw, pgb, g, b, alpha, tm, ec):
    t, d = x1.shape
    _, e, _, f = wg.shape
    dp = p.shape[2]
    expert = lambda i, c: (layer, c, 0, 0)
    const = lambda a: _layer_spec(a, layer, single_buffer=True)
    return pl.pallas_call(
        functools.partial(_moe_kernel, alpha=alpha),
        grid=(t // tm, e // ec),
        in_specs=[pl.BlockSpec((tm, d), lambda i, c: (i, 0)),
                  pl.BlockSpec((tm, e), lambda i, c: (i, 0)),
                  pl.BlockSpec((None, ec, d, f), expert),
                  pl.BlockSpec((None, ec, d, f), expert),
                  pl.BlockSpec((None, ec, f, d), expert),
                  const(sg), const(su), const(sd),
                  pl.BlockSpec((None, tm, dp), lambda i, c: (layer, i, 0)),
                  const(pw), const(pgw), const(pgb), const(g), const(b)],
        out_specs=pl.BlockSpec((tm, d), lambda i, c: (i, 0)),
        out_shape=jax.ShapeDtypeStruct((t, d), F32),
        scratch_shapes=[pltpu.VMEM((tm, d), BF16)],
        compiler_params=_params("parallel", "arbitrary"),
        name="moe",
    )(x1, comb, wg, wu, wd, sg, su, sd, p, pw, pgw, pgb, g, b)


def _pick_tile(n, target):
    tm = min(n, target)
    assert n % tm == 0, (n, tm)
    return tm


def kernel(x, p, w_in, conv_w, conv_b, lru_wa, lru_ba, lru_wx, lru_bx, lru_lambda, w_lru_out, s5_a_re, s5_a_im, s5_b_re, s5_b_im, s5_c_re, s5_c_im, s5_d, s5_log_dt, w_glu, b_glu, w_out, ln1_g, ln1_b, router_w, router_bias, moe_w_gate, moe_w_up, moe_w_down, shared_w_gate, shared_w_up, shared_w_down, ple_w, ple_gate_w, ple_gate_b, ln2_g, ln2_b):
    depth = w_in.shape[0]
    bsz, seq, d = x.shape
    t = bsz * seq
    d_rnn = conv_w.shape[-1]
    d_s5 = s5_d.shape[-1]
    alpha = (2.0 * depth) ** 0.25
    widths = (d_rnn, d_rnn, d_s5, d, d)
    assert sum(widths) == w_in.shape[-1]
    assert d_s5 % LANES == 0 and seq % S5_SUB == 0

    heads, dh = lru_wa.shape[1], lru_wa.shape[2]
    hpb = (LANES // math.gcd(dh, LANES))
    assert heads % hpb == 0
    ec = MOE_EXPERTS_PER_STEP
    tm_lru = _pick_tile(seq, TM_SEQ)
    tm_mix = _pick_tile(t, TM_MIX)
    tm_moe = _pick_tile(t, TM_MOE)
    n_pow = max(1, int(math.ceil(math.log2(seq // S5_SUB))))

    bf = lambda a: a.astype(BF16)
    w_in_b, w_lru_out_b, w_glu_b, w_out_b = bf(w_in), bf(w_lru_out), bf(w_glu), bf(w_out)
    wg_b, wu_b, wd_b = moe_w_gate, moe_w_up, moe_w_down
    sg_b, su_b, sd_b = bf(shared_w_gate), bf(shared_w_up), bf(shared_w_down)
    ple_w_b, ple_gate_w_b = bf(ple_w), bf(ple_gate_w)
    lru_params = (conv_w, _row3(conv_b), _block_diag_heads(lru_wa, hpb), _row3(lru_ba.reshape(depth, -1)),
                  _block_diag_heads(lru_wx, hpb), _row3(lru_bx.reshape(depth, -1)), _row3(lru_lambda), w_lru_out_b)
    mix_params = (w_glu_b, _row3(b_glu), w_out_b, _row3(ln1_g), _row3(ln1_b),
                  jnp.swapaxes(router_w, 1, 2), router_bias[..., None])

    groups, n_state = s5_a_re.shape[1], s5_a_re.shape[2]
    nc = d_s5 // LANES
    fold = lambda a: a.reshape((depth * groups,) + a.shape[2:])
    bd, s_re2, s_im2, cwc, (lre, lim) = _s5_operators(fold(s5_a_re), fold(s5_a_im), fold(s5_b_re), fold(s5_b_im),
                                                     fold(s5_c_re), fold(s5_c_im), fold(s5_log_dt))
    unfold = lambda a: a.reshape((depth, nc) + a.shape[1:])
    d_t = jnp.tile(s5_d.reshape(depth, nc, 1, LANES), (1, 1, 1, S5_SUB))
    s5_ops = (unfold(bd), unfold(s_re2), unfold(s_im2), unfold(cwc), unfold(_s5_wpow(lre, lim, n_pow)), d_t)
    assert groups == nc * S5_LANE_GROUPS and n_state * (LANES // n_state) == LANES

    h = x.reshape(t, d)
    p2 = p.reshape(depth, t, p.shape[-1])
    for i in range(depth):
        u_s5, gate_b, m_a = _inproj_lru(h, i, w_in_b, lru_params, widths, S5_SUB, bsz, tm_lru)
        ys = _s5_branch(u_s5, i, s5_ops, bsz)
        x1, comb = _mixout(ys, m_a, gate_b, h, i, mix_params, alpha, tm_mix)
        h = _moe(x1, comb, i, wg_b, wu_b, wd_b, sg_b, su_b, sd_b, p2, ple_w_b, ple_gate_w_b, _row3(ple_gate_b),
                 _row3(ln2_g), _row3(ln2_b), alpha, tm_moe, ec)
    return h.reshape(bsz, seq, d)
```

```python
import functools
import math

import jax
import jax.numpy as jnp
from jax import lax
from jax.experimental import pallas as pl
from jax.experimental.pallas import tpu as pltpu

F32 = jnp.float32
BF16 = jnp.bfloat16

LRU_C = 8.0
TOP_K = 8
N_EXPERT_GROUPS = 8
TOPK_GROUPS = 4
ROUTED_SCALE = 2.5
LN_EPS = 1e-5

LANES = 128
SUBLANES = 8
MXU_DIM = 256
PROJ_CHUNK = 2 * MXU_DIM
VMEM_LIMIT_BYTES = 62 * 1024 * 1024

S5_SUB = 16
S5_LANE_GROUPS = 8

TM_SEQ = 256
TM_MIX = 512
TM_MOE = 1024
MOE_EXPERTS_PER_STEP = 4


def _params(*sem):
    return pltpu.CompilerParams(dimension_semantics=sem, vmem_limit_bytes=VMEM_LIMIT_BYTES)


def _layer_spec(stacked, layer, single_buffer=False):
    nd = stacked.ndim - 1
    mode = {"pipeline_mode": pl.Buffered(1)} if single_buffer else {}
    return pl.BlockSpec((None,) + stacked.shape[1:], lambda *_: (layer,) + (0,) * nd, **mode)


def _row3(v):
    return v.reshape(v.shape[0], 1, -1)


def _gelu(x):
    return 0.5 * x * (1.0 + jnp.tanh(math.sqrt(2.0 / math.pi) * (x + 0.044715 * (x * x * x))))


def _sigmoid(x):
    return 1.0 / (1.0 + jnp.exp(-x))


def _layer_norm(v, g, b):
    mu = jnp.mean(v, axis=-1, keepdims=True)
    c = v - mu
    var = jnp.mean(c * c, axis=-1, keepdims=True)
    return c * lax.rsqrt(var + LN_EPS) * g + b


def _shift_rows(v, d, fill):
    n = v.shape[0]
    if d % SUBLANES == 0:
        head = jnp.broadcast_to(jnp.asarray(fill, v.dtype), (d,) + v.shape[1:])
        return jnp.concatenate([head, v[:n - d]], axis=0)
    row = lax.broadcasted_iota(jnp.int32, v.shape, 0)
    return jnp.where(row >= d, pltpu.roll(v, d, 0), fill)


def _scan_rows(a, b, h0, a_scr, b_scr, h_scr):
    n, c = a.shape
    ng = n // SUBLANES
    for ct in range(c // LANES):
        a_scr[ct] = a[:, ct * LANES:(ct + 1) * LANES]
        b_scr[ct] = b[:, ct * LANES:(ct + 1) * LANES]
    for ct in range(c // LANES):
        every = lambda s: pl.ds(s, ng, stride=SUBLANES)
        a_s = [a_scr[ct, every(s), :] for s in range(SUBLANES)]
        acum, hloc = [a_s[0]], [b_scr[ct, every(0), :]]
        for s in range(1, SUBLANES):
            hloc.append(a_s[s] * hloc[-1] + b_scr[ct, every(s), :])
            acum.append(a_s[s] * acum[-1])
        ga, gh = acum[-1], hloc[-1]
        d = 1
        while d < ng:
            gh = gh + ga * _shift_rows(gh, d, 0.0)
            ga = ga * _shift_rows(ga, d, 1.0)
            d *= 2
        h0c = h0[:, ct * LANES:(ct + 1) * LANES]
        after = gh + ga * h0c
        cin = _shift_rows(after, 1, h0c)
        for s in range(SUBLANES):
            h_scr[ct, every(s), :] = hloc[s] + acum[s] * cin
    return jnp.concatenate([h_scr[ct] for ct in range(c // LANES)], axis=1)


def _lru_tile(x_ref, g_ref, ga_ref, first, cw_ref, cb_ref, wa_ref, ba_ref, wx_ref, bx_ref, lam_ref, wo_ref,
              prev_ref, h_ref, a_scr, b_scr, hs_scr, n_blk, blk, side_work):
    sites = 5
    per_phase = -(-len(side_work) // sites)

    def run_side_work():
        for _ in range(min(per_phase, len(side_work))):
            side_work.pop(0)()

    tm = x_ref.shape[0]
    run_side_work()
    x = x_ref[...].astype(F32)
    ext = jnp.concatenate([prev_ref[...], x], axis=0)
    prev_ref[...] = x[tm - SUBLANES:, :]
    kw = cw_ref.shape[0]
    xc = cb_ref[...] + cw_ref[kw - 1:kw, :] * x
    for j in range(1, kw):
        xc = xc + cw_ref[kw - 1 - j:kw - j, :] * pltpu.roll(ext, j, 0)[SUBLANES:, :]

    run_side_work()
    xcb = xc.astype(BF16)
    r_parts, i_parts = [], []
    for k in range(n_blk):
        xk = xcb[:, k * blk:(k + 1) * blk]
        r_parts.append(jnp.dot(xk, wa_ref[k], preferred_element_type=F32))
        i_parts.append(jnp.dot(xk, wx_ref[k], preferred_element_type=F32))
    r = _sigmoid(jnp.concatenate(r_parts, axis=1) + ba_ref[...])
    ig = _sigmoid(jnp.concatenate(i_parts, axis=1) + bx_ref[...])

    run_side_work()
    nl = -lam_ref[...]
    softplus = jnp.maximum(nl, 0.0) + jnp.log1p(jnp.exp(-jnp.abs(nl)))
    log_a = (-LRU_C) * r * softplus
    a = jnp.exp(log_a)
    v = 1.0 - a * a
    mult = jnp.where(v > 0.0, v * lax.rsqrt(v), 0.0)
    row = lax.broadcasted_iota(jnp.int32, a.shape, 0)
    mult = jnp.where(jnp.logical_and(row == 0, first), 1.0, mult)
    b = mult * (ig * xc)

    run_side_work()
    h = _scan_rows(a, b, h_ref[0:1, :], a_scr, b_scr, hs_scr)
    h_ref[...] = jnp.broadcast_to(h[tm - 1:tm, :], h_ref.shape)

    run_side_work()
    y = (_gelu(g_ref[...].astype(F32)) * h).astype(BF16)
    ya = jnp.dot(y, wo_ref[...], preferred_element_type=F32)
    out = _sigmoid(ga_ref[...].astype(F32)) * ya
    while side_work:
        side_work.pop(0)()
    return out


def _inlru_kernel(x_ref, w_ref, cw_ref, cb_ref, wa_ref, ba_ref, wx_ref, bx_ref, lam_ref, wo_ref,
                  u_ref, gb_ref, ma_ref,
                  xl_scr, gl_scr, ga_scr, z_scr, prev_ref, h_ref, a_scr, b_scr, hs_scr, *, bounds, n_blk, blk):
    j = pl.program_id(1)

    @pl.when(j == 0)
    def _():
        xl_scr[...] = jnp.zeros_like(xl_scr)
        gl_scr[...] = jnp.zeros_like(gl_scr)
        ga_scr[...] = jnp.zeros_like(ga_scr)

    @pl.when(j <= 1)
    def _():
        prev_ref[...] = jnp.zeros_like(prev_ref)
        h_ref[...] = jnp.zeros_like(h_ref)

    fill, drain = j % 2, (j + 1) % 2
    xb = x_ref[...].astype(BF16)
    ts = u_ref.shape[2] // LANES

    def project(lo, hi, store):
        def thunk():
            store(jnp.dot(xb, w_ref[:, lo:hi], preferred_element_type=F32))
        return thunk

    def to_stash(scr, off):
        def store(z):
            scr[fill, :, off:off + z.shape[1]] = z.astype(scr.dtype)
        return store

    def to_gate_b(off):
        def store(z):
            gb_ref[:, off:off + z.shape[1]] = z.astype(gb_ref.dtype)
        return store

    def to_s5(off):
        def store(z):
            for k in range(z.shape[1] // LANES):
                c = off // LANES + k
                z_scr[c] = z[:, k * LANES:(k + 1) * LANES]
                for s in range(ts):
                    u_ref[c, :, s * LANES:(s + 1) * LANES] = z_scr[
                        c, pl.ds(s, u_ref.shape[1], stride=ts), :].astype(u_ref.dtype)
        return store

    sinks = (functools.partial(to_stash, xl_scr), functools.partial(to_stash, gl_scr), to_s5,
             functools.partial(to_stash, ga_scr), to_gate_b)
    side_work = []
    for (lo, hi), sink in zip(bounds, sinks):
        for c0 in range(lo, hi, PROJ_CHUNK):
            side_work.append(project(c0, min(c0 + PROJ_CHUNK, hi), sink(c0 - lo)))

    ma = _lru_tile(xl_scr.at[drain], gl_scr.at[drain], ga_scr.at[drain], j == 1,
                   cw_ref, cb_ref, wa_ref, ba_ref, wx_ref, bx_ref, lam_ref, wo_ref,
                   prev_ref, h_ref, a_scr, b_scr, hs_scr, n_blk, blk, side_work)
    ma_ref[...] = ma.astype(ma_ref.dtype)


def _inproj_lru(x, layer, w_all, lru_params, widths, ts, bsz, tm):
    t, d = x.shape
    d_rnn, _, d_s5, d_a, d_b = widths
    seq = t // bsz
    nt = seq // tm
    _, n_blk, blk, _ = lru_params[2].shape
    bounds, lo = [], 0
    for w in widths:
        bounds.append((lo, lo + w))
        lo += w
    cur = lambda b, j: (b * nt + jnp.minimum(j, nt - 1), 0)
    prv = lambda b, j: (b * nt + jnp.maximum(j - 1, 0), 0)
    nc = d_s5 // LANES
    return pl.pallas_call(
        functools.partial(_inlru_kernel, bounds=tuple(bounds), n_blk=n_blk, blk=blk),
        grid=(bsz, nt + 1),
        in_specs=[pl.BlockSpec((tm, d), cur), _layer_spec(w_all, layer, single_buffer=True)]
                 + [_layer_spec(a, layer) for a in lru_params],
        out_specs=[pl.BlockSpec((nc, tm // ts, ts * LANES), lambda b, j: (0,) + cur(b, j)),
                   pl.BlockSpec((tm, d_b), cur), pl.BlockSpec((tm, d_a), prv)],
        out_shape=[jax.ShapeDtypeStruct((nc, t // ts, ts * LANES), BF16),
                   jax.ShapeDtypeStruct((t, d_b), BF16), jax.ShapeDtypeStruct((t, d_a), BF16)],
        scratch_shapes=[pltpu.VMEM((2, tm, d_rnn), BF16), pltpu.VMEM((2, tm, d_rnn), BF16),
                        pltpu.VMEM((2, tm, d_a), BF16),
                        pltpu.VMEM((nc, tm, LANES), F32),
                        pltpu.VMEM((SUBLANES, d_rnn), F32), pltpu.VMEM((SUBLANES, d_rnn), F32)]
                       + [pltpu.VMEM((d_rnn // LANES, tm, LANES), F32)] * 3,
        compiler_params=_params("arbitrary", "arbitrary"),
        name="inproj_lru",
    )(x, w_all, *lru_params)


def _block_diag_heads(w, heads_per_blk):
    l, h, dh, _ = w.shape
    nb = h // heads_per_blk
    eye = jnp.eye(heads_per_blk, dtype=w.dtype)
    wb = w.reshape(l * nb, heads_per_blk, dh, dh)
    out = jnp.einsum("nhij,hk->nhikj", wb, eye)
    return out.reshape(l, nb, heads_per_blk * dh, heads_per_blk * dh).astype(BF16)


def _s5_lag_kernel(cr_ref, ci_ref, br_ref, bi_ref, pr_ref, pi_ref, k_ref):
    gb, p, n = cr_ref.shape
    ts = k_ref.shape[1]
    for g in range(gb):
        cr, ci = cr_ref[g], ci_ref[g]
        lag = lambda ref, j: ref[g, j:j + 1, :]
        are = jnp.concatenate([cr * lag(pr_ref, j) - ci * lag(pi_ref, j) for j in range(ts)], axis=0)
        aim = jnp.concatenate([cr * lag(pi_ref, j) + ci * lag(pr_ref, j) for j in range(ts)], axis=0)
        k = (jnp.dot(are, br_ref[g], preferred_element_type=F32, precision=lax.Precision.HIGHEST)
             - jnp.dot(aim, bi_ref[g], preferred_element_type=F32, precision=lax.Precision.HIGHEST))
        k_ref[g] = k.reshape(ts, p, p)


def _s5_lag_kernels(cr, ci, bbr, bbi, pwr, pwi):
    g, p, n = cr.shape
    ts = pwr.shape[1] - 1
    gb = S5_LANE_GROUPS
    cspec = pl.BlockSpec((gb, p, n), lambda i: (i, 0, 0))
    bspec = pl.BlockSpec((gb, n, p), lambda i: (i, 0, 0))
    pspec = pl.BlockSpec((gb, ts + 1, n), lambda i: (i, 0, 0))
    return pl.pallas_call(
        _s5_lag_kernel,
        grid=(g // gb,),
        in_specs=[cspec, cspec, bspec, bspec, pspec, pspec],
        out_specs=pl.BlockSpec((gb, ts, p, p), lambda i: (i, 0, 0, 0)),
        out_shape=jax.ShapeDtypeStruct((g, ts, p, p), F32),
        compiler_params=_params("parallel"),
        name="s5_lag_kernels",
    )(cr, ci, bbr, bbi, pwr, pwi)


def _s5_operators(a_re, a_im, b_re, b_im, c_re, c_im, log_dt):
    ts, gl = S5_SUB, S5_LANE_GROUPS
    g, n = a_re.shape
    p = b_re.shape[-1]
    nc = g // gl
    dt = jnp.exp(log_dt)[:, None]
    lre, lim = dt * a_re, dt * a_im
    mag = jnp.exp(lre)
    abar_re, abar_im = mag * jnp.cos(lim), mag * jnp.sin(lim)
    den = a_re * a_re + a_im * a_im
    z_re = ((abar_re - 1.0) * a_re + abar_im * a_im) / den
    z_im = (abar_im * a_re - (abar_re - 1.0) * a_im) / den
    bb_re = z_re[..., None] * b_re - z_im[..., None] * b_im
    bb_im = z_re[..., None] * b_im + z_im[..., None] * b_re
    steps = jnp.arange(ts + 1, dtype=F32)[None, :, None]
    pmag = jnp.exp(steps * lre[:, None, :])
    pw_re = pmag * jnp.cos(steps * lim[:, None, :])
    pw_im = pmag * jnp.sin(steps * lim[:, None, :])

    k = _s5_lag_kernels(c_re, c_im, bb_re, bb_im, pw_re, pw_im)
    bd = jnp.swapaxes(k, 2, 3).reshape(nc, gl, ts, p, p).transpose(0, 2, 1, 3, 4).reshape(nc, ts, gl * p, p)

    rev_re, rev_im = pw_re[:, ts - 1::-1][:, :ts], pw_im[:, ts - 1::-1][:, :ts]
    bt_re, bt_im = jnp.swapaxes(bb_re, 1, 2), jnp.swapaxes(bb_im, 1, 2)
    sb_re = rev_re[:, :, None, :] * bt_re[:, None] - rev_im[:, :, None, :] * bt_im[:, None]
    sb_im = rev_re[:, :, None, :] * bt_im[:, None] + rev_im[:, :, None, :] * bt_re[:, None]

    def rows_sgq(a):
        a = a.reshape(nc, gl, ts, p, n).transpose(0, 2, 1, 3, 4).reshape(nc, ts * gl * p, n)
        return jnp.concatenate([a] * (LANES // n), axis=-1)
    s_re2, s_im2 = rows_sgq(sb_re), rows_sgq(sb_im)

    c4 = lambda a: a.reshape(nc, gl, p, n).transpose(0, 3, 1, 2)[:, :, None]
    p4 = lambda a: a[:, 1:].reshape(nc, gl, ts, n).transpose(0, 3, 2, 1)[..., None]
    ct_re = c4(c_re) * p4(pw_re) - c4(c_im) * p4(pw_im)
    ct_im = c4(c_re) * p4(pw_im) + c4(c_im) * p4(pw_re)
    cwc = jnp.stack([ct_re, -ct_im], axis=1).reshape(nc, 2, n, ts * gl * p)
    return bd, s_re2, s_im2, cwc, (lre, lim)


def _s5_wpow(lre, lim, n_pow):
    ts, gl = S5_SUB, S5_LANE_GROUPS
    g, n = lre.shape
    nc = g // gl
    e = (ts * (2.0 ** jnp.arange(n_pow, dtype=F32)))[None, :, None]
    mag = jnp.exp(e * lre[:, None, :])
    wr = (mag * jnp.cos(e * lim[:, None, :])).reshape(nc, gl, n_pow, n)
    wi = (mag * jnp.sin(e * lim[:, None, :])).reshape(nc, gl, n_pow, n)
    w = jnp.stack([wr, wi], axis=0)
    return jnp.transpose(w, (1, 3, 0, 2, 4)).reshape(nc, n_pow, 2 * gl * n)


def _s5_assemble(bd_ref, sre_ref, sim_ref, cw_ref, m_scr, s_scr, c_scr):
    ts = bd_ref.shape[1]
    n = cw_ref.shape[2]
    half = s_scr.shape[1] // 2
    gl = half // n
    p = LANES // gl
    square = (LANES, LANES)
    spread = (lax.broadcasted_iota(jnp.int32, (p, LANES), 1) % p == lax.broadcasted_iota(jnp.int32, (p, LANES), 0))
    spread = jnp.where(spread, 1.0, 0.0).astype(m_scr.dtype)
    same_group = lax.broadcasted_iota(jnp.int32, square, 0) // p == lax.broadcasted_iota(jnp.int32, square, 1) // p
    lag_blocks = [jnp.where(same_group, jnp.dot(bd_ref[0, j].astype(m_scr.dtype), spread, preferred_element_type=F32),
                            0.0).astype(m_scr.dtype) for j in range(ts)]
    for s in range(ts):
        for t in range(ts):
            blk = (slice(s * LANES, (s + 1) * LANES), slice(t * LANES, (t + 1) * LANES))
            if t >= s:
                m_scr[blk] = lag_blocks[t - s]
            elif t * LANES // MXU_DIM == s * LANES // MXU_DIM:
                m_scr[blk] = jnp.zeros((LANES, LANES), m_scr.dtype)
    shape = (s_scr.shape[0], half)
    own = ((lax.broadcasted_iota(jnp.int32, shape, 0) // p) % gl) == (lax.broadcasted_iota(jnp.int32, shape, 1) // n)
    for ref, lo in ((sre_ref, 0), (sim_ref, half)):
        wide = jnp.concatenate([ref[0]] * (half // LANES), axis=1)
        s_scr[:, lo:lo + half] = jnp.where(own, wide, 0.0).astype(s_scr.dtype)
    colg = (lax.broadcasted_iota(jnp.int32, (n, c_scr.shape[1]), 1) // p) % gl
    for r in range(2):
        for g in range(gl):
            c_scr[r * half + g * n:r * half + (g + 1) * n, :] = jnp.where(colg == g, cw_ref[0, r], 0.0).astype(c_scr.dtype)


def _s5_state_scan(ds, w_ref, ds_scr, hs_scr, n_pow):
    rows, width = ds.shape
    half = width // 2
    nct = half // LANES
    ng = rows // SUBLANES
    assert rows <= 2 ** n_pow and rows % SUBLANES == 0
    for ct in range(2 * nct):
        ds_scr[ct] = ds[:, ct * LANES:(ct + 1) * LANES]
    every = lambda s: pl.ds(s, ng, stride=SUBLANES)
    cmul = lambda ar, ai, br, bi: (ar * br - ai * bi, ar * bi + ai * br)
    for ct in range(nct):
        cols = slice(ct * LANES, (ct + 1) * LANES)
        icols = slice(half + ct * LANES, half + (ct + 1) * LANES)
        w1r, w1i = w_ref[0, 0:1, cols], w_ref[0, 0:1, icols]
        hloc = [(ds_scr[ct, every(0), :], ds_scr[nct + ct, every(0), :])]
        for s in range(1, SUBLANES):
            pr, pi = cmul(w1r, w1i, *hloc[-1])
            hloc.append((pr + ds_scr[ct, every(s), :], pi + ds_scr[nct + ct, every(s), :]))
        gr, gi = hloc[-1]
        d, k = 1, 3
        while d < ng:
            sr, si = cmul(w_ref[0, k:k + 1, cols], w_ref[0, k:k + 1, icols],
                          _shift_rows(gr, d, 0.0), _shift_rows(gi, d, 0.0))
            gr, gi = gr + sr, gi + si
            d, k = 2 * d, k + 1
        cr, ci = _shift_rows(gr, 1, 0.0), _shift_rows(gi, 1, 0.0)
        hs_scr[ct, every(0), :] = cr
        hs_scr[nct + ct, every(0), :] = ci
        pr, pi = w1r, w1i
        for s in range(1, SUBLANES):
            ar, ai = cmul(pr, pi, cr, ci)
            hs_scr[ct, every(s), :] = hloc[s - 1][0] + ar
            hs_scr[nct + ct, every(s), :] = hloc[s - 1][1] + ai
            if s + 1 < SUBLANES:
                pr, pi = cmul(pr, pi, w1r, w1i)
    return jnp.concatenate([hs_scr[ct] for ct in range(2 * nct)], axis=1)


def _s5_kernel(x_ref, bd_ref, sre_ref, sim_ref, cw_ref, w_ref, d_ref, y_ref, m_scr, s_scr, c_scr, ds_scr, hs_scr,
               *, n_pow):
    @pl.when(pl.program_id(1) == 0)
    def _():
        _s5_assemble(bd_ref, sre_ref, sim_ref, cw_ref, m_scr, s_scr, c_scr)

    x = x_ref[0]
    ds = jnp.dot(x, s_scr[...], preferred_element_type=F32)
    hs = _s5_state_scan(ds, w_ref, ds_scr, hs_scr, n_pow).astype(BF16)
    for j in range(x.shape[1] // MXU_DIM):
        lo, hi_ = j * MXU_DIM, (j + 1) * MXU_DIM
        y = (jnp.dot(x[:, :hi_], m_scr[:hi_, lo:hi_], preferred_element_type=F32)
             + jnp.dot(hs, c_scr[:, lo:hi_], preferred_element_type=F32))
        y = y + d_ref[0, :, lo:hi_] * x[:, lo:hi_].astype(F32)
        y_ref[0, :, lo:hi_] = _gelu(y).astype(y_ref.dtype)


def _s5_branch(u2, layer, ops, bsz):
    nc, n_sub, width = u2.shape
    rows = n_sub // bsz
    cwc, wpow = ops[3], ops[4]
    n_state2 = 2 * S5_LANE_GROUPS * cwc.shape[3]
    x_spec = pl.BlockSpec((1, rows, width), lambda c, b: (c, b, 0))
    op_spec = lambda a: pl.BlockSpec((None, 1) + a.shape[2:], lambda c, b: (layer, c) + (0,) * (a.ndim - 2))
    return pl.pallas_call(
        functools.partial(_s5_kernel, n_pow=wpow.shape[2]),
        grid=(nc, bsz),
        in_specs=[x_spec] + [op_spec(a) for a in ops],
        out_specs=x_spec,
        out_shape=jax.ShapeDtypeStruct(u2.shape, BF16),
        scratch_shapes=[pltpu.VMEM((width, width), BF16), pltpu.VMEM((width, n_state2), BF16),
                        pltpu.VMEM((n_state2, width), BF16)]
                       + [pltpu.VMEM((n_state2 // LANES, rows, LANES), F32)] * 2,
        compiler_params=_params("arbitrary", "arbitrary"),
        name="s5_branch",
    )(u2, *ops)


def _route(logits_t, bias):
    e, tm = logits_t.shape
    per = e // N_EXPERT_GROUPS
    scores = _sigmoid(logits_t)
    biased = scores + bias
    g3 = biased.reshape(N_EXPERT_GROUPS, per, tm)
    sub = lax.broadcasted_iota(jnp.int32, g3.shape, 1)
    m1 = jnp.max(g3, axis=1, keepdims=True)
    first = jnp.min(jnp.where(g3 == m1, sub, per), axis=1, keepdims=True)
    m2 = jnp.max(jnp.where(sub == first, -jnp.inf, g3), axis=1, keepdims=True)
    gs = (m1 + m2).reshape(N_EXPERT_GROUPS, tm)
    gidx = lax.broadcasted_iota(jnp.int32, gs.shape, 0)
    grank = jnp.zeros(gs.shape, jnp.int32)
    for j in range(N_EXPERT_GROUPS):
        other = gs[j:j + 1, :]
        ahead = jnp.logical_or(other > gs, jnp.logical_and(other == gs, j < gidx))
        grank = grank + ahead.astype(jnp.int32)
    gsel = (grank < TOPK_GROUPS).reshape(N_EXPERT_GROUPS, 1, tm)
    masked = jnp.where(gsel, g3, -jnp.inf).reshape(e, tm)
    eidx = lax.broadcasted_iota(jnp.int32, masked.shape, 0)
    work, sel = masked, None
    for _ in range(TOP_K):
        top = jnp.max(work, axis=0, keepdims=True)
        cand = work == top if sel is None else jnp.logical_and(work == top, jnp.logical_not(sel))
        first = jnp.min(jnp.where(cand, eidx, e), axis=0, keepdims=True)
        pick = eidx == first
        sel = pick if sel is None else jnp.logical_or(sel, pick)
        work = jnp.where(pick, -jnp.inf, work)
    gate = jnp.where(sel, scores, 0.0)
    denom = jnp.sum(gate, axis=0, keepdims=True)
    return ROUTED_SCALE * gate / denom


def _mixout_kernel(ys_ref, ma_ref, gb_ref, x_ref, wglu_ref, bglu_ref, wout_ref, g_ref, b_ref, rw_ref, rb_ref,
                   x1_ref, comb_ref, ys_scr, *, alpha):
    d = x_ref.shape[1]
    nc, n_sub, width = ys_ref.shape
    ts = width // LANES
    for c in range(nc):
        for s in range(ts):
            ys_scr[c, pl.ds(s, n_sub, stride=ts), :] = ys_ref[c, :, s * LANES:(s + 1) * LANES].astype(F32)
    ys = jnp.concatenate([ys_scr[c] for c in range(nc)], axis=1).astype(BF16)
    glu = jnp.dot(ys, wglu_ref[...], preferred_element_type=F32) + bglu_ref[...]
    yb = glu[:, :d] * _sigmoid(glu[:, d:])
    merged = ma_ref[...].astype(F32) + _sigmoid(gb_ref[...].astype(F32)) * yb
    mix = jnp.dot(merged.astype(BF16), wout_ref[...], preferred_element_type=F32)
    x1 = _layer_norm(alpha * x_ref[...] + mix, g_ref[...], b_ref[...])
    x1_ref[...] = x1
    rw = rw_ref[...]
    rw_hi = rw.astype(BF16)
    rw_lo = (rw - rw_hi.astype(F32)).astype(BF16)
    x_hi = x1.astype(BF16)
    x_lo = (x1 - x_hi.astype(F32)).astype(BF16)
    nt = (((1,), (1,)), ((), ()))
    logits_t = (lax.dot_general(rw_hi, x_hi, nt, preferred_element_type=F32)
                + lax.dot_general(rw_hi, x_lo, nt, preferred_element_type=F32)
                + lax.dot_general(rw_lo, x_hi, nt, preferred_element_type=F32))
    comb_ref[...] = _route(logits_t, rb_ref[...]).T


def _mixout(ys, ma, gb, x, layer, params, alpha, tm):
    t, d = x.shape
    e = params[5].shape[1]
    nc, _, width = ys.shape
    ts = width // LANES
    row = lambda i: (i, 0)
    return pl.pallas_call(
        functools.partial(_mixout_kernel, alpha=alpha),
        grid=(t // tm,),
        in_specs=[pl.BlockSpec((nc, tm // ts, width), lambda i: (0, i, 0))] +
                 [pl.BlockSpec((tm, d), row)] * 3 + [_layer_spec(a, layer) for a in params],
        out_specs=[pl.BlockSpec((tm, d), row), pl.BlockSpec((tm, e), row)],
        out_shape=[jax.ShapeDtypeStruct((t, d), F32), jax.ShapeDtypeStruct((t, e), F32)],
        scratch_shapes=[pltpu.VMEM((nc, tm, LANES), F32)],
        compiler_params=_params("parallel"),
        name="mixout",
    )(ys, ma, gb, x, *params)


def _moe_kernel(x_ref, comb_ref, wg_ref, wu_ref, wd_ref, sg_ref, su_ref, sd_ref, p_ref, pw_ref, pgw_ref, pgb_ref,
                g_ref, b_ref, o_ref, xb_ref, *, alpha):
    c = pl.program_id(1)
    ec = wg_ref.shape[0]

    half = o_ref.shape[1] // 2

    def gate_up(xb, wg, wu):
        return (jnp.dot(xb, wg.astype(BF16), preferred_element_type=F32),
                jnp.dot(xb, wu.astype(BF16), preferred_element_type=F32))

    def add_ffn(gu, wd_ref_e, scale):
        gg, uu = gu
        hdn = gg * _sigmoid(gg) * uu
        if scale is not None:
            hdn = hdn * scale
        hdn = hdn.astype(BF16)
        for lo in (0, half):
            o_ref[:, lo:lo + half] += jnp.dot(hdn, wd_ref_e[:, lo:lo + half].astype(BF16),
                                              preferred_element_type=F32)

    @pl.when(c == 0)
    def _():
        x = x_ref[...]
        xb = x.astype(BF16)
        xb_ref[...] = xb
        gate = _sigmoid(jnp.dot(xb, pgw_ref[...], preferred_element_type=F32) + pgb_ref[...])
        ple = gate * jnp.dot(p_ref[...].astype(BF16), pw_ref[...], preferred_element_type=F32)
        o_ref[...] = alpha * x + ple
        add_ffn(gate_up(xb, sg_ref[...], su_ref[...]), sd_ref, None)

    xb = xb_ref[...]
    comb = comb_ref[...]
    lane = lax.broadcasted_iota(jnp.int32, comb.shape, 1)
    gu = gate_up(xb, wg_ref[0], wu_ref[0])
    for e in range(ec):
        gu_next = gate_up(xb, wg_ref[e + 1], wu_ref[e + 1]) if e + 1 < ec else None
        col = jnp.sum(jnp.where(lane == c * ec + e, comb, 0.0), axis=1, keepdims=True)
        add_ffn(gu, wd_ref.at[e], col)
        gu = gu_next

    @pl.when(c == pl.num_programs(1) - 1)
    def _():
        o_ref[...] = _layer_norm(o_ref[...], g_ref[...], b_ref[...])


def _moe(x1, comb, layer, wg, wu, wd, sg, su, sd, p, pw, pgw, pgb, g, b, alpha, tm, ec):
    t, d = x1.shape
    _, e, _, f = wg.shape
    dp = p.shape[2]
    expert = lambda i, c: (layer, c, 0, 0)
    const = lambda a: _layer_spec(a, layer, single_buffer=True)
    return pl.pallas_call(
        functools.partial(_moe_kernel, alpha=alpha),
        grid=(t // tm, e // ec),
        in_specs=[pl.BlockSpec((tm, d), lambda i, c: (i, 0)),
                  pl.BlockSpec((tm, e), lambda i, c: (i, 0)),
                  pl.BlockSpec((None, ec, d, f), expert),
                  pl.BlockSpec((None, ec, d, f), expert),
                  pl.BlockSpec((None, ec, f, d), expert),
                  const(sg), const(su), const(sd),
                  pl.BlockSpec((None, tm, dp), lambda i, c: (layer, i, 0)),
                  const(pw), const(pgw), const(pgb), const(g), const(b)],
        out_specs=pl.BlockSpec((tm, d), lambda i, c: (i, 0)),
        out_shape=jax.ShapeDtypeStruct((t, d), F32),
        scratch_shapes=[pltpu.VMEM((tm, d), BF16)],
        compiler_params=_params("parallel", "arbitrary"),
        name="moe",
    )(x1, comb, wg, wu, wd, sg, su, sd, p, pw, pgw, pgb, g, b)


def _pick_tile(n, target):
    tm = min(n, target)
    assert n % tm == 0, (n, tm)
    return tm


def kernel(x, p, w_in, conv_w, conv_b, lru_wa, lru_ba, lru_wx, lru_bx, lru_lambda, w_lru_out, s5_a_re, s5_a_im, s5_b_re, s5_b_im, s5_c_re, s5_c_im, s5_d, s5_log_dt, w_glu, b_glu, w_out, ln1_g, ln1_b, router_w, router_bias, moe_w_gate, moe_w_up, moe_w_down, shared_w_gate, shared_w_up, shared_w_down, ple_w, ple_gate_w, ple_gate_b, ln2_g, ln2_b):
    depth = w_in.shape[0]
    bsz, seq, d = x.shape
    t = bsz * seq
    d_rnn = conv_w.shape[-1]
    d_s5 = s5_d.shape[-1]
    alpha = (2.0 * depth) ** 0.25
    widths = (d_rnn, d_rnn, d_s5, d, d)
    assert sum(widths) == w_in.shape[-1]
    assert d_s5 % LANES == 0 and seq % S5_SUB == 0

    heads, dh = lru_wa.shape[1], lru_wa.shape[2]
    hpb = (LANES // math.gcd(dh, LANES))
    assert heads % hpb == 0
    ec = MOE_EXPERTS_PER_STEP
    tm_lru = _pick_tile(seq, TM_SEQ)
    tm_mix = _pick_tile(t, TM_MIX)
    tm_moe = _pick_tile(t, TM_MOE)
    n_pow = max(1, int(math.ceil(math.log2(seq // S5_SUB))))

    bf = lambda a: a.astype(BF16)
    w_in_b, w_lru_out_b, w_glu_b, w_out_b = bf(w_in), bf(w_lru_out), bf(w_glu), bf(w_out)
    wg_b, wu_b, wd_b = moe_w_gate, moe_w_up, moe_w_down
    sg_b, su_b, sd_b = bf(shared_w_gate), bf(shared_w_up), bf(shared_w_down)
    ple_w_b, ple_gate_w_b = bf(ple_w), bf(ple_gate_w)
    lru_params = (conv_w, _row3(conv_b), _block_diag_heads(lru_wa, hpb), _row3(lru_ba.reshape(depth, -1)),
                  _block_diag_heads(lru_wx, hpb), _row3(lru_bx.reshape(depth, -1)), _row3(lru_lambda), w_lru_out_b)
    mix_params = (w_glu_b, _row3(b_glu), w_out_b, _row3(ln1_g), _row3(ln1_b),
                  jnp.swapaxes(router_w, 1, 2), router_bias[..., None])

    groups, n_state = s5_a_re.shape[1], s5_a_re.shape[2]
    nc = d_s5 // LANES
    fold = lambda a: a.reshape((depth * groups,) + a.shape[2:])
    bd, s_re2, s_im2, cwc, (lre, lim) = _s5_operators(fold(s5_a_re), fold(s5_a_im), fold(s5_b_re), fold(s5_b_im),
                                                     fold(s5_c_re), fold(s5_c_im), fold(s5_log_dt))
    unfold = lambda a: a.reshape((depth, nc) + a.shape[1:])
    d_t = jnp.tile(s5_d.reshape(depth, nc, 1, LANES), (1, 1, 1, S5_SUB))
    s5_ops = (unfold(bd), unfold(s_re2), unfold(s_im2), unfold(cwc), unfold(_s5_wpow(lre, lim, n_pow)), d_t)
    assert groups == nc * S5_LANE_GROUPS and n_state * (LANES // n_state) == LANES

    h = x.reshape(t, d)
    p2 = p.reshape(depth, t, p.shape[-1])
    for i in range(depth):
        u_s5, gate_b, m_a = _inproj_lru(h, i, w_in_b, lru_params, widths, S5_SUB, bsz, tm_lru)
        ys = _s5_branch(u_s5, i, s5_ops, bsz)
        x1, comb = _mixout(ys, m_a, gate_b, h, i, mix_params, alpha, tm_mix)
        h = _moe(x1, comb, i, wg_b, wu_b, wd_b, sg_b, su_b, sd_b, p2, ple_w_b, ple_gate_w_b, _row3(ple_gate_b),
                 _row3(ln2_g), _row3(ln2_b), alpha, tm_moe, ec)
    return h.reshape(bsz, seq, d)
```
